```python
import math
import jax, jax.numpy as jnp
from jax import lax
import numpy as np

D_MODEL = 2048
BATCH = 2
SEQ = 4096
DEPTH = 4
DEC_BATCH = 128
DEC_SEQ = 8
PAST_LEN = 8192
PAGE_SIZE = 128

HEAD_V = 64
H_DIFF = 8
DK_DIFF = 32
H_MLA = 12
Q_LORA = 384
KV_LORA = 128
D_NOPE = 64
D_ROPE = 32
H_NSA = 12
DK_NSA = 64
CMP_BLOCK = 32
CMP_STRIDE = 16
SEL_BLOCK = 64
N_SEL = 16
WINDOW = 512
N_BRANCH = 3
MIX = (H_DIFF + H_MLA + H_NSA) * HEAD_V
N_EXPERTS = 32
TOP_K = 4
D_FF = 512
SWIGLU_LIMIT = 7.0
SWIGLU_ALPHA = 1.702
MOE_BLOCK = 128
N_BUCKETS = 32
MAX_DISTANCE = 128
QBLOCK = 128
ROPE_BASE = 10000.0
EPS = 1e-6
FORCE_SCORE = 1e9
DIFF_ROW = 2 * DK_DIFF + HEAD_V
MLA_ROW = KV_LORA + D_ROPE
NSA_ROW = DK_NSA + HEAD_V
IN_SPLITS = (H_DIFF * DK_DIFF, H_DIFF * DK_DIFF, DK_DIFF, DK_DIFF, HEAD_V,
             Q_LORA, KV_LORA, D_ROPE,
             H_NSA * DK_NSA, DK_NSA, HEAD_V, DK_NSA, HEAD_V, DK_NSA, HEAD_V, H_NSA * N_BRANCH)
P_IN = sum(IN_SPLITS)

kernel_name = 'hymba_diff_mla_nsa_moe_step'


def rmsnorm(x, g):
    xf = x.astype(jnp.float32)
    y = xf * lax.rsqrt(jnp.mean(xf * xf, axis=-1, keepdims=True) + EPS)
    return (y * g.astype(jnp.float32)).astype(x.dtype)


def rel_bucket(dist):
    dist = jnp.maximum(dist, 0)
    exact = N_BUCKETS // 2
    large = exact + (jnp.log(jnp.maximum(dist, 1).astype(jnp.float32) / exact)
                     / math.log(MAX_DISTANCE / exact) * (N_BUCKETS - exact)).astype(jnp.int32)
    return jnp.where(dist < exact, dist, jnp.minimum(large, N_BUCKETS - 1))


def masked_softmax(s, mask):
    s = jnp.where(mask, s.astype(jnp.float32), -jnp.inf)
    m = jnp.max(s, axis=-1, keepdims=True)
    m = jnp.where(jnp.isfinite(m), m, 0.0)
    p = jnp.exp(s - m)
    return p / jnp.maximum(jnp.sum(p, axis=-1, keepdims=True), 1e-30)


def rope(x, pos):
    half = x.shape[-1] // 2
    inv = ROPE_BASE ** (-jnp.arange(half, dtype=jnp.float32) / half)
    ang = pos.astype(jnp.float32)[:, None] * inv
    shape = (ang.shape[0],) + (1,) * (x.ndim - 3) + (half,)
    cos, sin = jnp.cos(ang).reshape(shape), jnp.sin(ang).reshape(shape)
    xf = x.astype(jnp.float32)
    x1, x2 = xf[..., :half], xf[..., half:]
    return jnp.concatenate([x1 * cos - x2 * sin, x2 * cos + x1 * sin], axis=-1).astype(x.dtype)


def map_qblocks(fn, qpos, *qs):
    T = qpos.shape[0]
    if T > QBLOCK and T % QBLOCK == 0:
        nb = T // QBLOCK
        def split(a):
            return a.reshape((a.shape[0], nb, QBLOCK) + a.shape[2:]).swapaxes(0, 1)
        xs = (qpos.reshape(nb, QBLOCK),) + tuple(split(a) for a in qs)
        out = lax.map(lambda args: fn(*args), xs)
        def merge(o):
            o = o.swapaxes(0, 1)
            return o.reshape((o.shape[0], T) + o.shape[3:])
        return jax.tree_util.tree_map(merge, out)
    return fn(qpos, *qs)


def gather_pages(cache, l, page_table):
    rows = cache[l, page_table]
    return rows.reshape(rows.shape[0], -1, rows.shape[-1])


def diff_attention(q1, q2, k1, k2, v, qpos, lam, bias_tab):
    kpos = jnp.arange(k1.shape[1], dtype=jnp.int32)
    scale = DK_DIFF ** -0.5
    def block(qp, a1, a2):
        dist = qp[:, None] - kpos[None, :]
        mask = dist >= 0
        bias = jnp.moveaxis(bias_tab[rel_bucket(dist)], -1, 0)
        s1 = jnp.einsum('bthd,bsd->bhts', a1, k1) * scale + bias
        s2 = jnp.einsum('bthd,bsd->bhts', a2, k2) * scale + bias
        p = masked_softmax(s1, mask) - lam * masked_softmax(s2, mask)
        return jnp.einsum('bhts,bsd->bthd', p.astype(v.dtype), v)
    return map_qblocks(block, qpos, q1, q2)


def mla_attention(q_lat, q_rope, ckv, kr, qpos):
    kpos = jnp.arange(ckv.shape[1], dtype=jnp.int32)
    scale = (D_NOPE + D_ROPE) ** -0.5
    def block(qp, ql, qr):
        mask = qp[:, None] >= kpos[None, :]
        s = (jnp.einsum('bthc,bsc->bhts', ql, ckv) + jnp.einsum('bthr,bsr->bhts', qr, kr)) * scale
        p = masked_softmax(s, mask)
        return jnp.einsum('bhts,bsc->bthc', p.astype(ckv.dtype), ckv)
    return map_qblocks(block, qpos, q_lat, q_rope)


def nsa_attention(q, gates, kv_cmp, kv_slc, kv_win, win_pos0, qpos, w_c, bias_tab):
    B, L = kv_cmp.shape[:2]
    scale = DK_NSA ** -0.5
    kc, vc = kv_cmp[..., :DK_NSA], kv_cmp[..., DK_NSA:]
    ks, vs = kv_slc[..., :DK_NSA], kv_slc[..., DK_NSA:]
    n16 = L // CMP_STRIDE
    n_cmp = n16 - CMP_BLOCK // CMP_STRIDE + 1
    def compress(a, w):
        ch = a[:, :n16 * CMP_STRIDE].reshape(B, n16, CMP_STRIDE, a.shape[-1])
        out = jnp.einsum('bnjd,j->bnd', ch[:, :n_cmp], w[:CMP_STRIDE])
        for o in range(1, CMP_BLOCK // CMP_STRIDE):
            out = out + jnp.einsum('bnjd,j->bnd', ch[:, o:o + n_cmp], w[o * CMP_STRIDE:(o + 1) * CMP_STRIDE])
        return out
    kc_t, vc_t = compress(kc, w_c[0]), compress(vc, w_c[1])
    cmp_end = jnp.arange(n_cmp, dtype=jnp.int32) * CMP_STRIDE + CMP_BLOCK - 1
    n_sel = -(-L // SEL_BLOCK)
    pad = n_sel * SEL_BLOCK - L
    ks_b = jnp.pad(ks, ((0, 0), (0, pad), (0, 0))).reshape(B, n_sel, SEL_BLOCK, DK_NSA)
    vs_b = jnp.pad(vs, ((0, 0), (0, pad), (0, 0))).reshape(B, n_sel, SEL_BLOCK, HEAD_V)
    n_top = min(N_SEL, n_sel)
    sel_blk = jnp.arange(n_sel, dtype=jnp.int32)
    agg_idx = ((sel_blk * SEL_BLOCK - CMP_BLOCK) // CMP_STRIDE + 1)[:, None] + jnp.arange((SEL_BLOCK + CMP_BLOCK) // CMP_STRIDE - 1, dtype=jnp.int32)
    agg_ok = (agg_idx >= 0) & (agg_idx < n_cmp)
    agg_idx = jnp.clip(agg_idx, 0, n_cmp - 1)
    tok = jnp.arange(SEL_BLOCK, dtype=jnp.int32)
    kw_pad = jnp.pad(kv_win, ((0, 0), (WINDOW, 0), (0, 0)))
    def block(qp, qb):
        tq = qb.shape[1]
        dist_c = qp[:, None] - cmp_end[None, :]
        bias_c = jnp.moveaxis(bias_tab[rel_bucket(dist_c)], -1, 0)
        s_c = jnp.einsum('bthd,bnd->bhtn', qb, kc_t) * scale + bias_c
        p_c = masked_softmax(s_c, dist_c >= 0)
        o_c = jnp.einsum('bhtn,bnd->bthd', p_c.astype(vc_t.dtype), vc_t)
        imp = p_c.sum(axis=1)
        imp = jnp.sum(jnp.where(agg_ok, imp[..., agg_idx], 0.0), axis=-1)
        cur = (qp // SEL_BLOCK)[:, None]
        forced = (sel_blk == 0) | (sel_blk == cur) | (sel_blk == cur - 1)
        imp = jnp.where(forced, FORCE_SCORE, imp)
        imp = jnp.where(sel_blk <= cur, imp, -1.0)
        _, idx = lax.top_k(imp, n_top)
        kg = jax.vmap(lambda kb, ix: kb[ix])(ks_b, idx).reshape(B, tq, n_top * SEL_BLOCK, DK_NSA)
        vg = jax.vmap(lambda vb, ix: vb[ix])(vs_b, idx).reshape(B, tq, n_top * SEL_BLOCK, HEAD_V)
        pos = (idx[..., None] * SEL_BLOCK + tok).reshape(B, tq, n_top * SEL_BLOCK)
        dist_s = qp[None, :, None] - pos
        bias_s = jnp.swapaxes(bias_tab[rel_bucket(dist_s)], -1, -2)
        s_s = jnp.einsum('bthd,btkd->bthk', qb, kg) * scale + bias_s
        p_s = masked_softmax(s_s, (dist_s >= 0)[:, :, None, :])
        o_s = jnp.einsum('bthk,btkd->bthd', p_s.astype(vg.dtype), vg)
        start = qp[0] - win_pos0
        kwin = lax.dynamic_slice_in_dim(kw_pad, start, WINDOW + tq, axis=1)
        kpos_w = (win_pos0 - WINDOW) + start + jnp.arange(WINDOW + tq, dtype=jnp.int32)
        dist_w = qp[:, None] - kpos_w[None, :]
        mask_w = (dist_w >= 0) & (dist_w < WINDOW) & (kpos_w >= win_pos0)[None, :]
        bias_w = jnp.moveaxis(bias_tab[rel_bucket(dist_w)], -1, 0)
        s_w = jnp.einsum('bthd,bsd->bhts', qb, kwin[..., :DK_NSA]) * scale + bias_w
        p_w = masked_softmax(s_w, mask_w)
        o_w = jnp.einsum('bhts,bsd->bthd', p_w.astype(kwin.dtype), kwin[..., DK_NSA:])
        return o_c, o_s, o_w
    o_c, o_s, o_w = map_qblocks(block, qpos, q)
    return gates[..., 0:1] * o_c + gates[..., 1:2] * o_s + gates[..., 2:3] * o_w


def moe(h, l, w_router, b_router, w_gate_up, b_gate_up, w_down, b_down):
    n_tok, d = h.shape
    n_assign = n_tok * TOP_K
    logits = (h @ w_router[l]).astype(jnp.float32) + b_router[l].astype(jnp.float32)
    top_v, top_e = lax.top_k(logits, TOP_K)
    gate = jax.nn.softmax(top_v, axis=-1).reshape(-1)
    flat_e = top_e.reshape(-1)
    flat_t = jnp.broadcast_to(jnp.arange(n_tok, dtype=jnp.int32)[:, None], (n_tok, TOP_K)).reshape(-1)
    order = jnp.argsort(flat_e)
    se, st, sg = flat_e[order], flat_t[order], gate[order]
    counts = jnp.bincount(flat_e, length=N_EXPERTS)
    padded = (counts + MOE_BLOCK - 1) // MOE_BLOCK * MOE_BLOCK
    pend = jnp.cumsum(padded)
    dest = (pend - padded)[se] + jnp.arange(n_assign, dtype=jnp.int32) - (jnp.cumsum(counts) - counts)[se]
    n_blocks = -(-(n_assign + N_EXPERTS * (MOE_BLOCK - 1)) // MOE_BLOCK)
    xd = jnp.zeros((n_blocks * MOE_BLOCK, d), h.dtype).at[dest].set(h[st])
    blk_e = jnp.minimum(jnp.searchsorted(pend, jnp.arange(n_blocks, dtype=jnp.int32) * MOE_BLOCK, side='right'), N_EXPERTS - 1)
    def expert_block(args):
        xb, e = args
        gu = xb @ w_gate_up[l, e] + b_gate_up[l, e]
        g = jnp.minimum(gu[:, :D_FF], SWIGLU_LIMIT)
        u = jnp.clip(gu[:, D_FF:], -SWIGLU_LIMIT, SWIGLU_LIMIT)
        return ((u + 1.0) * g * jax.nn.sigmoid(SWIGLU_ALPHA * g)) @ w_down[l, e] + b_down[l, e]
    yd = lax.map(expert_block, (xd.reshape(n_blocks, MOE_BLOCK, d), blk_e)).reshape(-1, d)
    return jnp.zeros_like(h).at[st].add(yd[dest] * sg[:, None].astype(h.dtype))


def trunk(x, c, past, page_table, weights):
    (w_mod, b_mod, g_attn, g_mlp, w_in, g_q_lat, g_kv_lat, w_uq, w_uk, w_uv,
     diff_lambda, g_diff_sub, w_cmp, w_out, w_router, b_router, w_gate_up,
     b_gate_up, w_down, b_down, rel_bias, g_final) = weights
    B, T, D = x.shape
    past_len = 0 if past is None else page_table.shape[1] * PAGE_SIZE
    qpos = past_len + jnp.arange(T, dtype=jnp.int32)
    offsets = [int(o) for o in np.cumsum(IN_SPLITS)[:-1]]
    bias_diff, bias_nsa = rel_bias[:, :H_DIFF], rel_bias[:, H_DIFF:]
    c_act = jax.nn.silu(c)
    new_diff, new_mla, new_cmp, new_slc, new_win = [], [], [], [], []
    for l in range(DEPTH):
        mod = c_act @ w_mod[l] + b_mod[l]
        sh1, sc1, ga1, sh2, sc2, ga2 = jnp.split(mod[:, None, :], 6, axis=-1)
        h = rmsnorm(x, g_attn[l]) * (1.0 + sc1) + sh1
        (q1, q2, k1, k2, va, cq, ckv, kr, qn, kcm, vcm, ksl, vsl, kwi, vwi, gt) = jnp.split(h @ w_in[l], offsets, axis=-1)
        ckv = rmsnorm(ckv, g_kv_lat[l])
        kr = rope(kr, qpos)
        row_diff = jnp.concatenate([k1, k2, va], axis=-1)
        row_mla = jnp.concatenate([ckv, kr], axis=-1)
        row_cmp = jnp.concatenate([kcm, vcm], axis=-1)
        row_slc = jnp.concatenate([ksl, vsl], axis=-1)
        row_win = jnp.concatenate([kwi, vwi], axis=-1)
        if past is None:
            full_diff, full_mla, full_cmp, full_slc, win = row_diff, row_mla, row_cmp, row_slc, row_win
            keep = min(WINDOW, T)
            win_pos0 = 0
        else:
            cache_diff, cache_mla, cache_cmp, cache_slc, state_win = past
            full_diff = jnp.concatenate([gather_pages(cache_diff, l, page_table), row_diff], axis=1)
            full_mla = jnp.concatenate([gather_pages(cache_mla, l, page_table), row_mla], axis=1)
            full_cmp = jnp.concatenate([gather_pages(cache_cmp, l, page_table), row_cmp], axis=1)
            full_slc = jnp.concatenate([gather_pages(cache_slc, l, page_table), row_slc], axis=1)
            win = jnp.concatenate([state_win[l], row_win], axis=1)
            keep = state_win.shape[2]
            win_pos0 = past_len - keep
        new_diff.append(row_diff)
        new_mla.append(row_mla)
        new_cmp.append(row_cmp)
        new_slc.append(row_slc)
        new_win.append(win[:, win.shape[1] - keep:])
        lam_init = 0.8 - 0.6 * math.exp(-0.3 * l)
        lq = diff_lambda[l].astype(jnp.float32)
        lam = jnp.exp(jnp.sum(lq[0] * lq[1])) - jnp.exp(jnp.sum(lq[2] * lq[3])) + lam_init
        o_diff = diff_attention(q1.reshape(B, T, H_DIFF, DK_DIFF), q2.reshape(B, T, H_DIFF, DK_DIFF),
                                full_diff[..., :DK_DIFF], full_diff[..., DK_DIFF:2 * DK_DIFF],
                                full_diff[..., 2 * DK_DIFF:], qpos, lam, bias_diff)
        o_diff = rmsnorm(o_diff, g_diff_sub[l]) * (1.0 - lam_init)
        qh = jnp.einsum('btc,chd->bthd', rmsnorm(cq, g_q_lat[l]), w_uq[l])
        q_rope = rope(qh[..., D_NOPE:], qpos)
        q_lat = jnp.einsum('bthn,chn->bthc', qh[..., :D_NOPE], w_uk[l])
        o_lat = mla_attention(q_lat, q_rope, full_mla[..., :KV_LORA], full_mla[..., KV_LORA:], qpos)
        o_mla = jnp.einsum('bthc,chd->bthd', o_lat, w_uv[l])
        gates = jax.nn.sigmoid(gt.reshape(B, T, H_NSA, N_BRANCH))
        o_nsa = nsa_attention(qn.reshape(B, T, H_NSA, DK_NSA), gates, full_cmp, full_slc, win,
                              win_pos0, qpos, w_cmp[l], bias_nsa)
        o = jnp.concatenate([o_diff, o_mla, o_nsa], axis=2).reshape(B, T, MIX)
        x = x + ga1 * (o @ w_out[l])
        h = rmsnorm(x, g_mlp[l]) * (1.0 + sc2) + sh2
        x = x + ga2 * moe(h.reshape(B * T, D), l, w_router, b_router, w_gate_up, b_gate_up, w_down, b_down).reshape(B, T, D)
    y = rmsnorm(x, g_final)
    return y, (jnp.stack(new_diff), jnp.stack(new_mla), jnp.stack(new_cmp), jnp.stack(new_slc), jnp.stack(new_win))


def setup_inputs(seed: int = 0) -> dict:
    key = jax.random.key(seed)
    keys = jax.random.split(key, 32)
    def nrm(i, shape, scale=1.0):
        return scale * jax.random.normal(keys[i], shape, jnp.float32)
    def gain(i, shape):
        return 1.0 + nrm(i, shape, 0.02)
    n_pages = PAST_LEN // PAGE_SIZE
    n_pool = (DEC_BATCH * n_pages * 5) // 4
    win_keep = min(WINDOW, PAST_LEN)
    page_table = jax.random.permutation(keys[7], n_pool)[:DEC_BATCH * n_pages].reshape(DEC_BATCH, n_pages).astype(jnp.int32)
    D = D_MODEL
    return {
        'x_prompt': nrm(0, (BATCH, SEQ, D)),
        'x_sample': nrm(1, (DEC_BATCH, DEC_SEQ, D)),
        'cache_diff': nrm(2, (DEPTH, n_pool, PAGE_SIZE, DIFF_ROW)),
        'cache_mla': nrm(3, (DEPTH, n_pool, PAGE_SIZE, MLA_ROW)),
        'cache_cmp': nrm(4, (DEPTH, n_pool, PAGE_SIZE, NSA_ROW)),
        'cache_slc': nrm(5, (DEPTH, n_pool, PAGE_SIZE, NSA_ROW)),
        'state_win': nrm(6, (DEPTH, DEC_BATCH, win_keep, NSA_ROW)),
        'page_table': page_table,
        'c_prompt': nrm(8, (BATCH, D)),
        'c_sample': nrm(9, (DEC_BATCH, D)),
        'w_mod': nrm(10, (DEPTH, D, 6 * D), 0.5 * D ** -0.5),
        'b_mod': nrm(11, (DEPTH, 6 * D), 0.02),
        'g_attn': gain(12, (DEPTH, D)),
        'g_mlp': gain(13, (DEPTH, D)),
        'w_in': nrm(14, (DEPTH, D, P_IN), D ** -0.5),
        'g_q_lat': gain(15, (DEPTH, Q_LORA)),
        'g_kv_lat': gain(16, (DEPTH, KV_LORA)),
        'w_uq': nrm(17, (DEPTH, Q_LORA, H_MLA, D_NOPE + D_ROPE), Q_LORA ** -0.5),
        'w_uk': nrm(18, (DEPTH, KV_LORA, H_MLA, D_NOPE), KV_LORA ** -0.5),
        'w_uv': nrm(19, (DEPTH, KV_LORA, H_MLA, HEAD_V), KV_LORA ** -0.5),
        'diff_lambda': nrm(20, (DEPTH, 4, DK_DIFF), 0.1),
        'g_diff_sub': gain(21, (DEPTH, HEAD_V)),
        'w_cmp': 1.0 / CMP_BLOCK + nrm(22, (DEPTH, 2, CMP_BLOCK), 0.5 * CMP_BLOCK ** -0.5),
        'w_out': nrm(23, (DEPTH, MIX, D), MIX ** -0.5),
        'w_router': nrm(24, (DEPTH, D, N_EXPERTS), D ** -0.5),
        'b_router': nrm(25, (DEPTH, N_EXPERTS), 0.01),
        'w_gate_up': nrm(26, (DEPTH, N_EXPERTS, D, 2 * D_FF), D ** -0.5),
        'b_gate_up': nrm(27, (DEPTH, N_EXPERTS, 2 * D_FF), 0.02),
        'w_down': nrm(28, (DEPTH, N_EXPERTS, D_FF, D), D_FF ** -0.5),
        'b_down': nrm(29, (DEPTH, N_EXPERTS, D), 0.02),
        'rel_bias': nrm(30, (N_BUCKETS, H_DIFF + H_NSA), 0.3),
        'g_final': gain(31, (D,)),
    }


def reference(x_prompt, x_sample, cache_diff, cache_mla, cache_cmp, cache_slc, state_win, page_table,
              c_prompt, c_sample, w_mod, b_mod, g_attn, g_mlp, w_in, g_q_lat, g_kv_lat, w_uq, w_uk, w_uv,
              diff_lambda, g_diff_sub, w_cmp, w_out, w_router, b_router, w_gate_up, b_gate_up, w_down,
              b_down, rel_bias, g_final):
    weights = (w_mod, b_mod, g_attn, g_mlp, w_in, g_q_lat, g_kv_lat, w_uq, w_uk, w_uv,
               diff_lambda, g_diff_sub, w_cmp, w_out, w_router, b_router, w_gate_up,
               b_gate_up, w_down, b_down, rel_bias, g_final)
    y_prompt, rows_p = trunk(x_prompt, c_prompt, None, page_table, weights)
    y_sample, rows_s = trunk(x_sample, c_sample, (cache_diff, cache_mla, cache_cmp, cache_slc, state_win),
                             page_table, weights)
    diff_p, mla_p, cmp_p, slc_p, win_p = rows_p
    diff_s, mla_s, cmp_s, slc_s, win_s = rows_s
    return (y_prompt, y_sample, diff_p, mla_p, cmp_p, slc_p, win_p, diff_s, mla_s, cmp_s, slc_s, win_s)
```

```python
import functools
import math

import jax
import jax.numpy as jnp
from jax import lax
from jax.experimental import pallas as pl
from jax.experimental.pallas import tpu as pltpu

F32 = jnp.float32
BF16 = jnp.bfloat16
NEG = -1e30

HEAD_V = 64
H_DIFF = 8
DK_DIFF = 32
H_MLA = 12
Q_LORA = 384
KV_LORA = 128
D_NOPE = 64
D_ROPE = 32
H_NSA = 12
DK_NSA = 64
CMP_BLOCK = 32
CMP_STRIDE = 16
SEL_BLOCK = 64
N_SEL = 16
WINDOW = 512
N_BRANCH = 3
TOP_K = 4
SWIGLU_LIMIT = 7.0
SWIGLU_ALPHA = 1.702
N_BUCKETS = 32
MAX_DISTANCE = 128
ROPE_BASE = 10000.0
EPS = 1e-6
FORCE_SCORE = 1e9

LANES = 128
SUBLANES = 8
MOE_ROWS = 128
VMEM_LIMIT = 56 * 1024 * 1024

PROJ_SPLITS = (2 * H_DIFF * DK_DIFF, 128, Q_LORA, KV_LORA, H_NSA * DK_NSA, 128, 128, 128, 128)


def _cparams(**kw):
    kw.setdefault("vmem_limit_bytes", VMEM_LIMIT)
    return pltpu.CompilerParams(**kw)


def _pick_tile(n, pref):
    t = pref
    while n % t:
        t //= 2
    return t


def _rms(x, g):
    return x * lax.rsqrt(jnp.mean(x * x, axis=-1, keepdims=True) + EPS) * g


def _sigmoid(x):
    return 1.0 / (1.0 + jnp.exp(-x))


def _iota(shape, dim):
    return lax.broadcasted_iota(jnp.int32, shape, dim)


def _mod_kernel(c_ref, w_ref, b_ref, o_ref):
    c = c_ref[...]
    a = (c * _sigmoid(c)).astype(BF16)
    o_ref[...] = jnp.dot(a, w_ref[...].astype(BF16), preferred_element_type=F32) + b_ref[...]


def _mod_call(c_all, w_mod, b_mod):
    depth, d, n6 = w_mod.shape
    m = c_all.shape[0]
    tn = _pick_tile(n6, 1024)
    return pl.pallas_call(
        _mod_kernel,
        grid=(depth, n6 // tn),
        in_specs=[pl.BlockSpec((m, d), lambda l, j: (0, 0)),
                  pl.BlockSpec((None, d, tn), lambda l, j: (l, 0, j)),
                  pl.BlockSpec((None, 1, tn), lambda l, j: (l, 0, j))],
        out_specs=pl.BlockSpec((None, m, tn), lambda l, j: (l, 0, j)),
        out_shape=jax.ShapeDtypeStruct((depth, m, n6), F32),
        compiler_params=_cparams(),
        name="adaln_mod",
    )(c_all, w_mod, b_mod.reshape(depth, 1, n6))


def _proj_kernel(l_ref, x_ref, sh_ref, sc_ref, g_ref, w_ref, *outs):
    g8, _, d = x_ref.shape
    y = _rms(x_ref[...], g_ref[...])
    h = y * (1.0 + sc_ref[...]) + sh_ref[...]
    h = h.reshape(g8 * SUBLANES, d).astype(BF16)
    r = jnp.dot(h, w_ref[...], preferred_element_type=F32)
    off = 0
    for o in outs:
        wdt = o.shape[1]
        o[...] = r[:, off:off + wdt]
        off += wdt


def _proj_call(x3, modg, g_attn, w_in_p, lidx, tm):
    n8, _, d = x3.shape
    n = n8 * SUBLANES
    g8 = tm // SUBLANES
    p = w_in_p.shape[2]
    grid_spec = pltpu.PrefetchScalarGridSpec(
        num_scalar_prefetch=1,
        grid=(n // tm,),
        in_specs=[pl.BlockSpec((g8, SUBLANES, d), lambda i, l: (i, 0, 0)),
                  pl.BlockSpec((None, None, g8, 1, d), lambda i, l: (l[0], 0, i, 0, 0)),
                  pl.BlockSpec((None, None, g8, 1, d), lambda i, l: (l[0], 1, i, 0, 0)),
                  pl.BlockSpec((None, 1, d), lambda i, l: (l[0], 0, 0)),
                  pl.BlockSpec((None, d, p), lambda i, l: (l[0], 0, 0))],
        out_specs=[pl.BlockSpec((tm, w), lambda i, l: (i, 0)) for w in PROJ_SPLITS],
    )
    return pl.pallas_call(
        _proj_kernel,
        grid_spec=grid_spec,
        out_shape=[jax.ShapeDtypeStruct((n, w), F32) for w in PROJ_SPLITS],
        compiler_params=_cparams(),
        name="in_proj",
    )(lidx, x3, modg, modg, g_attn, w_in_p)


def _mla_prep_kernel(l_ref, cq_ref, ckv_ref, misc_ref, cc_ref, ss_ref, gq_ref, gkv_ref, wn_ref, wr_ref, wrs_ref,
                     wuk_ref, qabs_ref, row_ref):
    tm = cq_ref.shape[0]
    cqn = _rms(cq_ref[...], gq_ref[...]).astype(BF16)
    qn = jnp.dot(cqn, wn_ref[...], preferred_element_type=F32)
    qr = jnp.dot(cqn, wr_ref[...], preferred_element_type=F32)
    qrs = jnp.dot(cqn, wrs_ref[...], preferred_element_type=F32)
    cc = cc_ref[...]
    ss = ss_ref[...]
    per = LANES // D_ROPE
    roped = [qr[:, LANES * k:LANES * (k + 1)] * cc + qrs[:, LANES * k:LANES * (k + 1)] * ss
             for k in range(H_MLA // per)]
    wq = KV_LORA + LANES
    for h in range(H_MLA):
        ql = jnp.dot(qn[:, D_NOPE * h:D_NOPE * (h + 1)].astype(BF16), wuk_ref[h], preferred_element_type=F32)
        qabs_ref[:, wq * h:wq * h + KV_LORA] = ql
        qabs_ref[:, wq * h + KV_LORA:wq * (h + 1)] = _lane_pick(roped[h // per], D_ROPE * (h % per), D_ROPE, 0)
    misc = misc_ref[...]
    krr = misc * cc + pltpu.roll(misc, LANES - D_ROPE, 1) * ss
    row_ref[:, 0:KV_LORA] = _rms(ckv_ref[...], gkv_ref[...])
    row_ref[:, KV_LORA:KV_LORA + D_ROPE] = krr[:, 0:D_ROPE]


def _mla_prep_call(cq, ckv, misc, cc, ss, g_q, g_kv, wn, wr, wrs, wuk, lidx, tm):
    n = cq.shape[0]
    wq = KV_LORA + LANES
    row = lambda i, l: (i, 0)
    lay3 = lambda i, l: (l[0], 0, 0)
    grid_spec = pltpu.PrefetchScalarGridSpec(
        num_scalar_prefetch=1,
        grid=(n // tm,),
        in_specs=[pl.BlockSpec((tm, Q_LORA), row), pl.BlockSpec((tm, KV_LORA), row),
                  pl.BlockSpec((tm, LANES), row), pl.BlockSpec((tm, LANES), row), pl.BlockSpec((tm, LANES), row),
                  pl.BlockSpec((None, 1, Q_LORA), lay3), pl.BlockSpec((None, 1, KV_LORA), lay3),
                  pl.BlockSpec((None, Q_LORA, H_MLA * D_NOPE), lay3),
                  pl.BlockSpec((None, Q_LORA, H_MLA * D_ROPE), lay3),
                  pl.BlockSpec((None, Q_LORA, H_MLA * D_ROPE), lay3),
                  pl.BlockSpec((None, H_MLA, D_NOPE, KV_LORA), lambda i, l: (l[0], 0, 0, 0))],
        out_specs=[pl.BlockSpec((tm, H_MLA * wq), row), pl.BlockSpec((tm, KV_LORA + D_ROPE), row)],
    )
    return pl.pallas_call(
        _mla_prep_kernel,
        grid_spec=grid_spec,
        out_shape=[jax.ShapeDtypeStruct((n, H_MLA * wq), F32), jax.ShapeDtypeStruct((n, KV_LORA + D_ROPE), F32)],
        compiler_params=_cparams(),
        name="mla_prep",
    )(lidx, cq, ckv, misc, cc, ss, g_q, g_kv, wn, wr, wrs, wuk)


def _flash_init(n_maps, rows, width):
    return tuple((jnp.full((rows, 1), NEG, F32), jnp.zeros((rows, 1), F32), jnp.zeros((rows, width), F32))
                 for _ in range(n_maps))


def _flash_step(carry, qs, kk, vv, scale, bias=None, add=None, keep=None):
    out = []
    for (m, l, acc), q in zip(carry, qs):
        s = lax.dot_general(q, kk, (((1,), (1,)), ((), ())), preferred_element_type=F32) * scale
        if bias is not None:
            s = s + bias
        if add is not None:
            s = s + add
        if keep is not None:
            s = jnp.where(keep, s, NEG)
        m_new = jnp.maximum(m, jnp.max(s, axis=1, keepdims=True))
        alpha = jnp.exp(m - m_new)
        p = jnp.exp(s - m_new)
        l = alpha * l + jnp.sum(p, axis=1, keepdims=True)
        acc = alpha * acc + jnp.dot(p.astype(BF16), vv, preferred_element_type=F32)
        out.append((m_new, l, acc))
    return tuple(out)


def _bias_prompt(t_ref, h, blk0, tq, tk):
    rows = []
    for a in range(tq // LANES):
        cols = [t_ref[h, jnp.clip(blk0 + a - b, 0, 2)] for b in range(tk // LANES)]
        rows.append(jnp.concatenate(cols, axis=1) if len(cols) > 1 else cols[0])
    return jnp.concatenate(rows, axis=0) if len(rows) > 1 else rows[0]


def _bias_stacked(ts_ref, blk0, tk):
    cols = [ts_ref[jnp.clip(blk0 - b, 0, 2)] for b in range(tk // LANES)]
    return jnp.concatenate(cols, axis=1) if len(cols) > 1 else cols[0]


def _lane_pick(x, src, width, dst):
    shift = (dst - src) % LANES
    y = pltpu.roll(x, shift, 1) if shift else x
    lane = _iota(x.shape, 1)
    return jnp.where((lane >= dst) & (lane < dst + width), y, 0.0)


def _page_copies(pt_ref, cache_hbm, l, b, buf, sem, slot, n_pages, page, width):
    return [pltpu.make_async_copy(cache_hbm.at[l, pt_ref[b, p]],
                                  buf.at[slot, pl.ds(p * page, page), pl.ds(0, width)],
                                  sem.at[slot]) for p in range(n_pages)]


def _paged_fetch(pt_ref, cache_hbm, l, buf, sem, n_pages, page, width):
    b = pl.program_id(0)
    nb = pl.num_programs(0)
    slot = b % 2

    @pl.when(b == 0)
    def _():
        for cp in _page_copies(pt_ref, cache_hbm, l, 0, buf, sem, 0, n_pages, page, width):
            cp.start()

    @pl.when(b + 1 < nb)
    def _():
        for cp in _page_copies(pt_ref, cache_hbm, l, b + 1, buf, sem, 1 - slot, n_pages, page, width):
            cp.start()

    for cp in _page_copies(pt_ref, cache_hbm, l, b, buf, sem, slot, n_pages, page, width):
        cp.wait()
    return slot


def _tail_rows(new_rows, width):
    t = new_rows
    if t.shape[1] < width:
        t = jnp.concatenate([t, jnp.zeros((t.shape[0], width - t.shape[1]), F32)], axis=1)
    return jnp.concatenate([t, jnp.zeros((LANES - t.shape[0], width), F32)], axis=0).astype(BF16)


def _diff_lambda(lq_ref, li_ref):
    lq = lq_ref[...]
    a = jnp.exp(jnp.sum(lq[0:1] * lq[1:2], axis=1, keepdims=True))
    b = jnp.exp(jnp.sum(lq[2:3] * lq[3:4], axis=1, keepdims=True))
    return a - b + li_ref[...]


def _diff_finish(carry, lam, li, g128):
    (_, l1, a1), (_, l2, a2) = carry
    o = a1 / l1 - lam * (a2 / l2)
    lane = _iota(o.shape, 1)
    ms = jnp.sum(jnp.where(lane >= HEAD_V, o * o, 0.0), axis=1, keepdims=True) / HEAD_V
    return o * lax.rsqrt(ms + EPS) * g128 * (1.0 - li)


def _diff_prompt_kernel(l_ref, q_ref, kv_ref, t_ref, lq_ref, li_ref, g_ref, o_ref, kv_sc, *, tq, tk):
    i = pl.program_id(1)

    @pl.when(i == 0)
    def _():
        kv_sc[...] = kv_ref[...].astype(BF16)

    scale = DK_DIFF ** -0.5
    lam = _diff_lambda(lq_ref, li_ref)
    li = li_ref[...]
    g128 = g_ref[...]
    q0 = i * tq
    cd = q0 // tk
    qpos = q0 + _iota((tq, 1), 0)
    keep_d = qpos >= cd * tk + _iota((1, tk), 1)
    per = LANES // DK_DIFF
    hq = H_DIFF * DK_DIFF
    lane = _iota((tq, LANES), 1)
    prev = None
    for h in range(H_DIFF):
        blk, off = h // per, DK_DIFF * (h % per)
        q1 = _lane_pick(q_ref[:, LANES * blk:LANES * (blk + 1)], off, DK_DIFF, 0).astype(BF16)
        q2 = _lane_pick(q_ref[:, hq + LANES * blk:hq + LANES * (blk + 1)], off, DK_DIFF, DK_DIFF).astype(BF16)

        def body(c, carry, q1=q1, q2=q2, h=h):
            kk = kv_sc[pl.ds(pl.multiple_of(c * tk, tk), tk), :]
            bias = _bias_prompt(t_ref, h, (q0 - c * tk) // LANES, tq, tk)
            return _flash_step(carry, (q1, q2), kk, kk, scale, bias=bias)

        carry = lax.fori_loop(0, cd, body, _flash_init(2, tq, LANES))
        kk = kv_sc[pl.ds(pl.multiple_of(cd * tk, tk), tk), :]
        bias = _bias_prompt(t_ref, h, (q0 - cd * tk) // LANES, tq, tk)
        carry = _flash_step(carry, (q1, q2), kk, kk, scale, bias=bias, keep=keep_d)
        o = _diff_finish(carry, lam, li, g128)
        if h % 2 == 0:
            prev = o
        else:
            pair = jnp.where(lane < HEAD_V, pltpu.roll(prev, HEAD_V, 1), o)
            o_ref[:, LANES * (h // 2):LANES * (h // 2 + 1)] = pair


def _diff_prompt_call(qd, rdiff, t_diff, lq, li, g128, lidx, batch, seq, tq, tk):
    nq = seq // tq
    grid_spec = pltpu.PrefetchScalarGridSpec(
        num_scalar_prefetch=1,
        grid=(batch, nq),
        in_specs=[pl.BlockSpec((tq, 2 * H_DIFF * DK_DIFF), lambda b, i, l: (b * nq + i, 0)),
                  pl.BlockSpec((seq, LANES), lambda b, i, l: (b, 0)),
                  pl.BlockSpec(t_diff.shape, lambda b, i, l: (0, 0, 0, 0)),
                  pl.BlockSpec((None, 4, DK_DIFF), lambda b, i, l: (l[0], 0, 0)),
                  pl.BlockSpec((None, 1, 1), lambda b, i, l: (l[0], 0, 0)),
                  pl.BlockSpec((None, 1, LANES), lambda b, i, l: (l[0], 0, 0))],
        out_specs=pl.BlockSpec((tq, H_DIFF * HEAD_V), lambda b, i, l: (b * nq + i, 0)),
        scratch_shapes=[pltpu.VMEM((seq, LANES), BF16)],
    )
    return pl.pallas_call(
        functools.partial(_diff_prompt_kernel, tq=tq, tk=tk),
        grid_spec=grid_spec,
        out_shape=jax.ShapeDtypeStruct((batch * seq, H_DIFF * HEAD_V), F32),
        compiler_params=_cparams(),
        name="diff_attn_prompt",
    )(lidx, qd, rdiff, t_diff, lq, li, g128)


def _diff_sample_kernel(pt_ref, l_ref, q_ref, new_ref, ts_ref, lq_ref, li_ref, g_ref, cache_hbm, o_ref, buf, sem,
                        *, past, page, tk):
    n_pages = past // page
    slot = _paged_fetch(pt_ref, cache_hbm, l_ref[0], buf, sem, n_pages, page, LANES)
    rows = H_DIFF * SUBLANES
    scale = DK_DIFF ** -0.5
    lam = _diff_lambda(lq_ref, li_ref)
    q1 = q_ref[0:rows, :].astype(BF16)
    q2 = q_ref[rows:2 * rows, :].astype(BF16)

    def body(c, carry):
        kk = buf[slot, pl.ds(pl.multiple_of(c * tk, tk), tk), :].astype(BF16)
        bias = _bias_stacked(ts_ref, (past - c * tk) // LANES, tk)
        return _flash_step(carry, (q1, q2), kk, kk, scale, bias=bias)

    carry = lax.fori_loop(0, past // tk, body, _flash_init(2, rows, LANES))
    kt = _tail_rows(new_ref[...], LANES)
    t_in = _iota((rows, 1), 0) % SUBLANES
    keep = _iota((1, LANES), 1) <= t_in
    carry = _flash_step(carry, (q1, q2), kt, kt, scale, bias=ts_ref[0], keep=keep)
    o_ref[...] = _diff_finish(carry, lam, li_ref[...], g_ref[...])


def _diff_sample_call(q_st, rdiff, ts_diff, lq, li, g128, cache, page_table, lidx, row0, past, tk):
    nb = q_st.shape[0]
    page = cache.shape[2]
    rows = H_DIFF * SUBLANES
    grid_spec = pltpu.PrefetchScalarGridSpec(
        num_scalar_prefetch=2,
        grid=(nb,),
        in_specs=[pl.BlockSpec((None, 2 * rows, LANES), lambda b, pt, l: (b, 0, 0)),
                  pl.BlockSpec((SUBLANES, LANES), lambda b, pt, l: (row0 // SUBLANES + b, 0)),
                  pl.BlockSpec(ts_diff.shape, lambda b, pt, l: (0, 0, 0)),
                  pl.BlockSpec((None, 4, DK_DIFF), lambda b, pt, l: (l[0], 0, 0)),
                  pl.BlockSpec((None, 1, 1), lambda b, pt, l: (l[0], 0, 0)),
                  pl.BlockSpec((None, 1, LANES), lambda b, pt, l: (l[0], 0, 0)),
                  pl.BlockSpec(memory_space=pl.ANY)],
        out_specs=pl.BlockSpec((None, rows, LANES), lambda b, pt, l: (b, 0, 0)),
        scratch_shapes=[pltpu.VMEM((2, past, LANES), F32), pltpu.SemaphoreType.DMA((2,))],
    )
    return pl.pallas_call(
        functools.partial(_diff_sample_kernel, past=past, page=page, tk=tk),
        grid_spec=grid_spec,
        out_shape=jax.ShapeDtypeStruct((nb, rows, LANES), F32),
        compiler_params=_cparams(),
        name="diff_attn_sample",
    )(page_table, lidx, q_st, rdiff, ts_diff, lq, li, g128, cache)


MLA_W = KV_LORA + LANES


def _mla_prompt_kernel(q_ref, kv_ref, o_ref, kv_sc, *, tq, tk):
    i = pl.program_id(1)

    @pl.when(i == 0)
    def _():
        kv_sc[...] = jnp.zeros(kv_sc.shape, BF16)
        kv_sc[:, 0:KV_LORA + D_ROPE] = kv_ref[...].astype(BF16)

    scale = (D_NOPE + D_ROPE) ** -0.5
    q0 = i * tq
    cd = q0 // tk
    keep_d = (q0 + _iota((tq, 1), 0)) >= cd * tk + _iota((1, tk), 1)
    for h in range(H_MLA):
        q = q_ref[:, MLA_W * h:MLA_W * (h + 1)].astype(BF16)

        def body(c, carry, q=q):
            kk = kv_sc[pl.ds(pl.multiple_of(c * tk, tk), tk), :]
            return _flash_step(carry, (q,), kk, kk[:, 0:KV_LORA], scale)

        carry = lax.fori_loop(0, cd, body, _flash_init(1, tq, KV_LORA))
        kk = kv_sc[pl.ds(pl.multiple_of(cd * tk, tk), tk), :]
        ((_, l, acc),) = _flash_step(carry, (q,), kk, kk[:, 0:KV_LORA], scale, keep=keep_d)
        o_ref[:, KV_LORA * h:KV_LORA * (h + 1)] = acc / l


def _mla_prompt_call(qabs, rmla, batch, seq, tq, tk):
    nq = seq // tq
    return pl.pallas_call(
        functools.partial(_mla_prompt_kernel, tq=tq, tk=tk),
        grid=(batch, nq),
        in_specs=[pl.BlockSpec((tq, H_MLA * MLA_W), lambda b, i: (b * nq + i, 0)),
                  pl.BlockSpec((seq, KV_LORA + D_ROPE), lambda b, i: (b, 0))],
        out_specs=pl.BlockSpec((tq, H_MLA * KV_LORA), lambda b, i: (b * nq + i, 0)),
        out_shape=jax.ShapeDtypeStruct((batch * seq, H_MLA * KV_LORA), F32),
        scratch_shapes=[pltpu.VMEM((seq, MLA_W), BF16)],
        compiler_params=_cparams(),
        name="mla_attn_prompt",
    )(qabs, rmla)


def _mla_sample_kernel(pt_ref, l_ref, q_ref, new_ref, cache_hbm, o_ref, buf, sem, *, past, page, tk):
    n_pages = past // page
    row_w = KV_LORA + D_ROPE
    slot = _paged_fetch(pt_ref, cache_hbm, l_ref[0], buf, sem, n_pages, page, row_w)
    rows = H_MLA * SUBLANES
    scale = (D_NOPE + D_ROPE) ** -0.5
    q = q_ref[:, 0:row_w].astype(BF16)

    def body(c, carry):
        kk = buf[slot, pl.ds(pl.multiple_of(c * tk, tk), tk), :].astype(BF16)
        return _flash_step(carry, (q,), kk, kk[:, 0:KV_LORA], scale)

    carry = lax.fori_loop(0, past // tk, body, _flash_init(1, rows, KV_LORA))
    kt = _tail_rows(new_ref[...], row_w)
    keep = _iota((1, LANES), 1) <= (_iota((rows, 1), 0) % SUBLANES)
    ((_, l, acc),) = _flash_step(carry, (q,), kt, kt[:, 0:KV_LORA], scale, keep=keep)
    o_ref[...] = acc / l


def _mla_sample_call(q_st, rmla, cache, page_table, lidx, row0, past, tk):
    nb = q_st.shape[0]
    page = cache.shape[2]
    rows = H_MLA * SUBLANES
    grid_spec = pltpu.PrefetchScalarGridSpec(
        num_scalar_prefetch=2,
        grid=(nb,),
        in_specs=[pl.BlockSpec((None, rows, MLA_W), lambda b, pt, l: (b, 0, 0)),
                  pl.BlockSpec((SUBLANES, KV_LORA + D_ROPE), lambda b, pt, l: (row0 // SUBLANES + b, 0)),
                  pl.BlockSpec(memory_space=pl.ANY)],
        out_specs=pl.BlockSpec((None, rows, KV_LORA), lambda b, pt, l: (b, 0, 0)),
        scratch_shapes=[pltpu.VMEM((2, past, KV_LORA + D_ROPE), F32), pltpu.SemaphoreType.DMA((2,))],
    )
    return pl.pallas_call(
        functools.partial(_mla_sample_kernel, past=past, page=page, tk=tk),
        grid_spec=grid_spec,
        out_shape=jax.ShapeDtypeStruct((nb, rows, KV_LORA), F32),
        compiler_params=_cparams(),
        name="mla_attn_sample",
    )(page_table, lidx, q_st, rmla, cache)


def _compress(load_rows, w_ref, s1_sc, n16):
    s0 = jnp.zeros((n16, LANES), F32)
    s1 = jnp.zeros((n16, LANES), F32)
    for j in range(CMP_STRIDE):
        x = load_rows(j)
        s0 = s0 + x * w_ref[j:j + 1, :]
        s1 = s1 + x * w_ref[CMP_STRIDE + j:CMP_STRIDE + j + 1, :]
    s1_sc[0:n16, :] = s1
    s1_sc[n16:n16 + SUBLANES, :] = jnp.zeros((SUBLANES, LANES), F32)
    out = s0 + s1_sc[1:n16 + 1, :]
    return jnp.where(_iota((n16, 1), 0) < n16 - 1, out, 0.0)


def _compress_prompt_kernel(l_ref, kv_ref, w_ref, o_ref, s1_sc, *, n16):
    o_ref[...] = _compress(lambda j: kv_ref[pl.ds(j, n16, stride=CMP_STRIDE), :], w_ref, s1_sc, n16)


def _compress_prompt_call(rcmp, w_lane, lidx, batch, seq):
    n16 = seq // CMP_STRIDE
    grid_spec = pltpu.PrefetchScalarGridSpec(
        num_scalar_prefetch=1,
        grid=(batch,),
        in_specs=[pl.BlockSpec((seq, LANES), lambda b, l: (b, 0)),
                  pl.BlockSpec((None, CMP_BLOCK, LANES), lambda b, l: (l[0], 0, 0))],
        out_specs=pl.BlockSpec((None, n16, LANES), lambda b, l: (b, 0, 0)),
        scratch_shapes=[pltpu.VMEM((n16 + SUBLANES, LANES), F32)],
    )
    return pl.pallas_call(
        functools.partial(_compress_prompt_kernel, n16=n16),
        grid_spec=grid_spec,
        out_shape=jax.ShapeDtypeStruct((batch, n16, LANES), F32),
        compiler_params=_cparams(),
        name="nsa_compress_prompt",
    )(lidx, rcmp, w_lane)


def _compress_sample_kernel(pt_ref, l_ref, w_ref, cache_hbm, o_ref, buf, sem, s1_sc, *, past, page):
    slot = _paged_fetch(pt_ref, cache_hbm, l_ref[0], buf, sem, past // page, page, LANES)
    n16 = past // CMP_STRIDE
    o_ref[...] = _compress(lambda j: buf[slot, pl.ds(j, n16, stride=CMP_STRIDE), :], w_ref, s1_sc, n16)


def _compress_sample_call(w_lane, cache, page_table, lidx, nb, past):
    page = cache.shape[2]
    n16 = past // CMP_STRIDE
    grid_spec = pltpu.PrefetchScalarGridSpec(
        num_scalar_prefetch=2,
        grid=(nb,),
        in_specs=[pl.BlockSpec((None, CMP_BLOCK, LANES), lambda b, pt, l: (l[0], 0, 0)),
                  pl.BlockSpec(memory_space=pl.ANY)],
        out_specs=pl.BlockSpec((None, n16, LANES), lambda b, pt, l: (b, 0, 0)),
        scratch_shapes=[pltpu.VMEM((2, past, LANES), F32), pltpu.SemaphoreType.DMA((2,)),
                        pltpu.VMEM((n16 + SUBLANES, LANES), F32)],
    )
    return pl.pallas_call(
        functools.partial(_compress_sample_kernel, past=past, page=page),
        grid_spec=grid_spec,
        out_shape=jax.ShapeDtypeStruct((nb, n16, LANES), F32),
        compiler_params=_cparams(),
        name="nsa_compress_sample",
    )(page_table, lidx, w_lane, cache)


def _split3(x):
    hi = x.astype(BF16)
    r = x - hi.astype(F32)
    mid = r.astype(BF16)
    lo = (r - mid.astype(F32)).astype(BF16)
    return hi, mid, lo


def _select_blocks(imp_cmp, qpos, n_cmp, n_sel, n_sel_pad):
    rows, n16 = imp_cmp.shape
    ci = _iota((n16, n_sel_pad), 0)
    sj = _iota((n16, n_sel_pad), 1)
    span = (SEL_BLOCK + CMP_BLOCK) // CMP_STRIDE - 1
    first = sj * (SEL_BLOCK // CMP_STRIDE) - CMP_BLOCK // CMP_STRIDE + 1
    agg = jnp.where((ci >= first) & (ci < first + span) & (ci < n_cmp) & (sj < n_sel), 1.0, 0.0).astype(BF16)
    imp = sum(jnp.dot(part, agg, preferred_element_type=F32) for part in _split3(imp_cmp))
    blk = _iota((rows, n_sel_pad), 1)
    cur = qpos // SEL_BLOCK
    forced = (blk == 0) | (blk == cur) | (blk == cur - 1)
    imp = jnp.where(forced, FORCE_SCORE, imp)
    imp = jnp.where(blk <= cur, imp, -1.0)
    imp = jnp.where(blk < n_sel, imp, -2.0)
    sel = jnp.zeros((rows, n_sel_pad), F32)
    blk_f = blk.astype(F32)
    for _ in range(min(N_SEL, n_sel)):
        mx = jnp.max(imp, axis=1, keepdims=True)
        first_idx = jnp.min(jnp.where(imp == mx, blk_f, float(n_sel_pad)), axis=1, keepdims=True)
        pick = blk_f == first_idx
        sel = jnp.where(pick, 1.0, sel)
        imp = jnp.where(pick, -3.0, imp)
    return sel


def _cmp_softmax(s, valid):
    s = jnp.where(valid, s, NEG)
    m = jnp.max(s, axis=1, keepdims=True)
    p = jnp.where(valid, jnp.exp(s - m), 0.0)
    return p / jnp.maximum(jnp.sum(p, axis=1, keepdims=True), 1e-30)


def _cmp_prompt_kernel(q_ref, ct_ref, bias_ref, oc_ref, sel_ref, *, tq, n_cmp, n_sel):
    i = pl.program_id(1)
    n16 = ct_ref.shape[0]
    scale = DK_NSA ** -0.5
    kvc = ct_ref[...].astype(BF16)
    qpos = i * tq + _iota((tq, 1), 0)
    n_i = _iota((1, n16), 1)
    valid = (qpos >= n_i * CMP_STRIDE + CMP_BLOCK - 1) & (n_i < n_cmp)
    lane = _iota((tq, LANES), 1)
    imp = jnp.zeros((tq, n16), F32)
    prev = None
    for h in range(H_NSA):
        q = _lane_pick(q_ref[:, LANES * (h // 2):LANES * (h // 2 + 1)], DK_NSA * (h % 2), DK_NSA, 0).astype(BF16)
        s = lax.dot_general(q, kvc, (((1,), (1,)), ((), ())), preferred_element_type=F32) * scale + bias_ref[h]
        p = _cmp_softmax(s, valid)
        imp = imp + p
        o = jnp.dot(p.astype(BF16), kvc, preferred_element_type=F32)
        if h % 2 == 0:
            prev = o
        else:
            oc_ref[:, LANES * (h // 2):LANES * (h // 2 + 1)] = jnp.where(lane < HEAD_V, pltpu.roll(prev, HEAD_V, 1), o)
    sel_ref[...] = _select_blocks(imp, qpos, n_cmp, n_sel, sel_ref.shape[1])


def _cmp_prompt_call(qn, cmp_t, bias_c, batch, seq, tq, n_sel_pad):
    nq = seq // tq
    n16 = seq // CMP_STRIDE
    n_cmp = n16 - CMP_BLOCK // CMP_STRIDE + 1
    n_sel = -(-seq // SEL_BLOCK)
    return pl.pallas_call(
        functools.partial(_cmp_prompt_kernel, tq=tq, n_cmp=n_cmp, n_sel=n_sel),
        grid=(batch, nq),
        in_specs=[pl.BlockSpec((tq, H_NSA * DK_NSA), lambda b, i: (b * nq + i, 0)),
                  pl.BlockSpec((None, n16, LANES), lambda b, i: (b, 0, 0)),
                  pl.BlockSpec((H_NSA, tq, n16), lambda b, i: (0, i, 0))],
        out_specs=[pl.BlockSpec((tq, H_NSA * HEAD_V), lambda b, i: (b * nq + i, 0)),
                   pl.BlockSpec((tq, n_sel_pad), lambda b, i: (b * nq + i, 0))],
        out_shape=[jax.ShapeDtypeStruct((batch * seq, H_NSA * HEAD_V), F32),
                   jax.ShapeDtypeStruct((batch * seq, n_sel_pad), F32)],
        compiler_params=_cparams(),
        name="nsa_cmp_prompt",
    )(qn, cmp_t, bias_c)


def _cmp_sample_kernel(q_ref, ct_ref, bias_ref, oc_ref, sel_ref, *, past, n_cmp, n_sel):
    n16 = ct_ref.shape[0]
    rows = H_NSA * SUBLANES
    scale = DK_NSA ** -0.5
    kvc = ct_ref[...].astype(BF16)
    t_in = _iota((rows, 1), 0) % SUBLANES
    n_i = _iota((1, n16), 1)
    valid = (past + t_in >= n_i * CMP_STRIDE + CMP_BLOCK - 1) & (n_i < n_cmp)
    q = q_ref[...].astype(BF16)
    s = lax.dot_general(q, kvc, (((1,), (1,)), ((), ())), preferred_element_type=F32) * scale + bias_ref[...]
    p = _cmp_softmax(s, valid)
    oc_ref[...] = jnp.dot(p.astype(BF16), kvc, preferred_element_type=F32)
    imp = p[0:SUBLANES]
    for h in range(1, H_NSA):
        imp = imp + p[SUBLANES * h:SUBLANES * (h + 1)]
    qpos = past + _iota((SUBLANES, 1), 0)
    sel_ref[...] = _select_blocks(imp, qpos, n_cmp, n_sel, sel_ref.shape[1])


def _cmp_sample_call(q_st, cmp_t, bias_cs, past, seq_new, n_sel_pad):
    nb = q_st.shape[0]
    rows = H_NSA * SUBLANES
    n16 = (past + seq_new) // CMP_STRIDE
    n_cmp = n16 - CMP_BLOCK // CMP_STRIDE + 1
    n_sel = -(-(past + seq_new) // SEL_BLOCK)
    return pl.pallas_call(
        functools.partial(_cmp_sample_kernel, past=past, n_cmp=n_cmp, n_sel=n_sel),
        grid=(nb,),
        in_specs=[pl.BlockSpec((None, rows, LANES), lambda b: (b, 0, 0)),
                  pl.BlockSpec((None, n16, LANES), lambda b: (b, 0, 0)),
                  pl.BlockSpec((rows, n16), lambda b: (0, 0))],
        out_specs=[pl.BlockSpec((None, rows, LANES), lambda b: (b, 0, 0)),
                   pl.BlockSpec((None, SUBLANES, n_sel_pad), lambda b: (b, 0, 0))],
        out_shape=[jax.ShapeDtypeStruct((nb, rows, LANES), F32),
                   jax.ShapeDtypeStruct((nb, SUBLANES, n_sel_pad), F32)],
        compiler_params=_cparams(),
        name="nsa_cmp_sample",
    )(q_st, cmp_t, bias_cs)


def _sel_add_mask(sel_bf, kpos0, tk, causal_keep=None):
    n_sel_pad = sel_bf.shape[1]
    blk = (kpos0 + _iota((n_sel_pad, tk), 1)) // SEL_BLOCK
    expand = jnp.where(_iota((n_sel_pad, tk), 0) == blk, 1.0, 0.0).astype(BF16)
    hit = jnp.dot(sel_bf, expand, preferred_element_type=F32) > 0.5
    if causal_keep is not None:
        hit = hit & causal_keep
    return jnp.where(hit, 0.0, NEG)


def _nsa_prompt_kernel(q_ref, ks_ref, kw_ref, t_ref, sel_ref, oc_ref, misc_ref, o_ref, ks_sc, kw_sc, mask_sc,
                       *, tq, tk):
    i = pl.program_id(1)

    @pl.when(i == 0)
    def _():
        ks_sc[...] = ks_ref[...].astype(BF16)
        kw_sc[...] = kw_ref[...].astype(BF16)

    scale = DK_NSA ** -0.5
    q0 = i * tq
    cd = q0 // tk
    qpos = q0 + _iota((tq, 1), 0)
    sel_bf = sel_ref[...].astype(BF16)
    n_chunks = mask_sc.shape[0]
    for c in range(n_chunks):
        @pl.when(c <= cd)
        def _(c=c):
            keep = qpos >= c * tk + _iota((1, tk), 1)
            mask_sc[c] = _sel_add_mask(sel_bf, c * tk, tk, keep)

    gates = _sigmoid(misc_ref[...])
    goff = 2 * D_ROPE
    lane = _iota((tq, LANES), 1)
    win_chunks = []
    for cw in (cd - 1, cd):
        kp = cw * tk + _iota((1, tk), 1)
        win_chunks.append((jnp.maximum(cw, 0), cw, (kp <= qpos) & (kp > qpos - WINDOW) & (kp >= 0)))
    prev = None
    for h in range(H_NSA):
        q = _lane_pick(q_ref[:, LANES * (h // 2):LANES * (h // 2 + 1)], DK_NSA * (h % 2), DK_NSA, 0).astype(BF16)

        def body(c, carry, q=q, h=h):
            kk = ks_sc[pl.ds(pl.multiple_of(c * tk, tk), tk), :]
            bias = _bias_prompt(t_ref, h, (q0 - c * tk) // LANES, tq, tk)
            return _flash_step(carry, (q,), kk, kk, scale, bias=bias, add=mask_sc[c])

        ((_, ls, accs),) = lax.fori_loop(0, cd + 1, body, _flash_init(1, tq, LANES))
        carry = _flash_init(1, tq, LANES)
        for cc, cw, keep in win_chunks:
            kk = kw_sc[pl.ds(pl.multiple_of(cc * tk, tk), tk), :]
            bias = _bias_prompt(t_ref, h, (q0 - cw * tk) // LANES, tq, tk)
            carry = _flash_step(carry, (q,), kk, kk, scale, bias=bias, keep=keep)
        ((_, lw, accw),) = carry
        oc = oc_ref[:, LANES * (h // 2):LANES * (h // 2 + 1)]
        if h % 2 == 0:
            oc = pltpu.roll(oc, HEAD_V, 1)
        g0 = gates[:, goff + N_BRANCH * h:goff + N_BRANCH * h + 1]
        g1 = gates[:, goff + N_BRANCH * h + 1:goff + N_BRANCH * h + 2]
        g2 = gates[:, goff + N_BRANCH * h + 2:goff + N_BRANCH * h + 3]
        o = g0 * oc + g1 * (accs / ls) + g2 * (accw / lw)
        if h % 2 == 0:
            prev = o
        else:
            o_ref[:, LANES * (h // 2):LANES * (h // 2 + 1)] = jnp.where(lane < HEAD_V, pltpu.roll(prev, HEAD_V, 1), o)


def _nsa_prompt_call(qn, rslc, rwin, t_nsa, sel, oc, misc, batch, seq, tq, tk):
    nq = seq // tq
    n_sel_pad = sel.shape[1]
    row = lambda b, i: (b * nq + i, 0)
    return pl.pallas_call(
        functools.partial(_nsa_prompt_kernel, tq=tq, tk=tk),
        grid=(batch, nq),
        in_specs=[pl.BlockSpec((tq, H_NSA * DK_NSA), row),
                  pl.BlockSpec((seq, LANES), lambda b, i: (b, 0)),
                  pl.BlockSpec((seq, LANES), lambda b, i: (b, 0)),
                  pl.BlockSpec(t_nsa.shape, lambda b, i: (0, 0, 0, 0)),
                  pl.BlockSpec((tq, n_sel_pad), row),
                  pl.BlockSpec((tq, H_NSA * HEAD_V), row),
                  pl.BlockSpec((tq, LANES), row)],
        out_specs=pl.BlockSpec((tq, H_NSA * HEAD_V), row),
        out_shape=jax.ShapeDtypeStruct((batch * seq, H_NSA * HEAD_V), F32),
        scratch_shapes=[pltpu.VMEM((seq, LANES), BF16), pltpu.VMEM((seq, LANES), BF16),
                        pltpu.VMEM((seq // tk, tq, tk), F32)],
        compiler_params=_cparams(),
        name="nsa_attn_prompt",
    )(qn, rslc, rwin, t_nsa, sel, oc, misc)


def _nsa_sample_kernel(pt_ref, l_ref, q_ref, news_ref, neww_ref, win_ref, ts_ref, sel_ref, oc_ref, gt_ref,
                       cache_hbm, o_ref, buf, sem, *, past, page, tk):
    slot = _paged_fetch(pt_ref, cache_hbm, l_ref[0], buf, sem, past // page, page, LANES)
    rows = H_NSA * SUBLANES
    scale = DK_NSA ** -0.5
    q = q_ref[...].astype(BF16)
    sel_bf = sel_ref[...].astype(BF16)
    t_in = _iota((rows, 1), 0) % SUBLANES
    qpos = past + t_in
    keep_tail = _iota((1, LANES), 1) <= t_in

    def body(c, carry):
        kk = buf[slot, pl.ds(pl.multiple_of(c * tk, tk), tk), :].astype(BF16)
        bias = _bias_stacked(ts_ref, (past - c * tk) // LANES, tk)
        return _flash_step(carry, (q,), kk, kk, scale, bias=bias, add=_sel_add_mask(sel_bf, c * tk, tk))

    carry = lax.fori_loop(0, past // tk, body, _flash_init(1, rows, LANES))
    kt = _tail_rows(news_ref[...], LANES)
    ((_, ls, accs),) = _flash_step(carry, (q,), kt, kt, scale, bias=ts_ref[0],
                                   add=_sel_add_mask(sel_bf, past, LANES, keep_tail))

    lw = win_ref.shape[0]
    kbase = past - lw
    carry = _flash_init(1, rows, LANES)
    for c in range(lw // tk):
        kk = win_ref[c * tk:(c + 1) * tk, :].astype(BF16)
        kp = kbase + c * tk + _iota((1, tk), 1)
        bias = _bias_stacked(ts_ref, (past - kbase - c * tk) // LANES, tk)
        carry = _flash_step(carry, (q,), kk, kk, scale, bias=bias, keep=kp > qpos - WINDOW)
    kt = _tail_rows(neww_ref[...], LANES)
    ((_, lwn, accw),) = _flash_step(carry, (q,), kt, kt, scale, bias=ts_ref[0], keep=keep_tail)

    gates = _sigmoid(gt_ref[...])
    o_ref[...] = gates[:, 0:1] * oc_ref[...] + gates[:, 1:2] * (accs / ls) + gates[:, 2:3] * (accw / lwn)


def _nsa_sample_call(q_st, rslc, rwin, state_win, ts_nsa, sel_st, oc_st, gt_st, cache, page_table, lidx, row0,
                     past, tk):
    nb = q_st.shape[0]
    page = cache.shape[2]
    rows = H_NSA * SUBLANES
    lw = state_win.shape[2]
    n_sel_pad = sel_st.shape[2]
    st3 = lambda b, pt, l: (b, 0, 0)
    new = lambda b, pt, l: (row0 // SUBLANES + b, 0)
    grid_spec = pltpu.PrefetchScalarGridSpec(
        num_scalar_prefetch=2,
        grid=(nb,),
        in_specs=[pl.BlockSpec((None, rows, LANES), st3),
                  pl.BlockSpec((SUBLANES, LANES), new),
                  pl.BlockSpec((SUBLANES, LANES), new),
                  pl.BlockSpec((None, None, lw, LANES), lambda b, pt, l: (l[0], b, 0, 0)),
                  pl.BlockSpec(ts_nsa.shape, lambda b, pt, l: (0, 0, 0)),
                  pl.BlockSpec((None, rows, n_sel_pad), st3),
                  pl.BlockSpec((None, rows, LANES), st3),
                  pl.BlockSpec((None, rows, LANES), st3),
                  pl.BlockSpec(memory_space=pl.ANY)],
        out_specs=pl.BlockSpec((None, rows, LANES), st3),
        scratch_shapes=[pltpu.VMEM((2, past, LANES), F32), pltpu.SemaphoreType.DMA((2,))],
    )
    return pl.pallas_call(
        functools.partial(_nsa_sample_kernel, past=past, page=page, tk=tk),
        grid_spec=grid_spec,
        out_shape=jax.ShapeDtypeStruct((nb, rows, LANES), F32),
        compiler_params=_cparams(),
        name="nsa_attn_sample",
    )(page_table, lidx, q_st, rslc, rwin, state_win, ts_nsa, sel_st, oc_st, gt_st, cache)


def _post_attn_kernel(l_ref, x_ref, od_ref, ol_ref, on_ref, ga_ref, sh_ref, sc_ref, g_ref, wuv_ref, wo_ref, wr_ref, br_ref,
                      x1_ref, h2_ref, eidx_ref, gate_ref):
    g8, _, d = x_ref.shape
    tm = g8 * SUBLANES
    ol = ol_ref[...]
    parts = [od_ref[...]]
    for hp in range(H_MLA // 2):
        parts.append(jnp.dot(ol[:, 2 * KV_LORA * hp:2 * KV_LORA * (hp + 1)].astype(BF16), wuv_ref[hp],
                             preferred_element_type=F32))
    parts.append(on_ref[...])
    o = jnp.concatenate(parts, axis=1).astype(BF16)
    y = jnp.dot(o, wo_ref[...], preferred_element_type=F32).reshape(g8, SUBLANES, d)
    x1 = x_ref[...] + ga_ref[...] * y
    x1_ref[...] = x1
    h2 = _rms(x1, g_ref[...]) * (1.0 + sc_ref[...]) + sh_ref[...]
    h2 = h2.reshape(tm, d)
    h2_ref[...] = h2
    logits = jnp.dot(h2.astype(BF16), wr_ref[...], preferred_element_type=F32) + br_ref[...]
    lane = _iota((tm, LANES), 1)
    lane_f = lane.astype(F32)
    eidx = jnp.zeros((tm, LANES), F32)
    vals = jnp.full((tm, LANES), NEG, F32)
    work = logits
    for k in range(TOP_K):
        mx = jnp.max(work, axis=1, keepdims=True)
        idx = jnp.min(jnp.where(work == mx, lane_f, float(LANES)), axis=1, keepdims=True)
        eidx = jnp.where(lane == k, idx, eidx)
        vals = jnp.where(lane == k, mx, vals)
        work = jnp.where(lane_f == idx, NEG, work)
    e = jnp.exp(vals - jnp.max(vals, axis=1, keepdims=True))
    gate_ref[...] = e / jnp.sum(e, axis=1, keepdims=True)
    eidx_ref[...] = eidx.astype(jnp.int32)


def _post_attn_call(x3, od, ol, on, modg, g_mlp, wuv, wo, wr, br, lidx, tm):
    n8, _, d = x3.shape
    n = n8 * SUBLANES
    g8 = tm // SUBLANES
    row = lambda i, l: (i, 0)
    lay3 = lambda i, l: (l[0], 0, 0)
    mod = lambda k: pl.BlockSpec((None, None, g8, 1, d), lambda i, l: (l[0], k, i, 0, 0))
    grid_spec = pltpu.PrefetchScalarGridSpec(
        num_scalar_prefetch=1,
        grid=(n // tm,),
        in_specs=[pl.BlockSpec((g8, SUBLANES, d), lambda i, l: (i, 0, 0)),
                  pl.BlockSpec((tm, od.shape[1]), row), pl.BlockSpec((tm, ol.shape[1]), row),
                  pl.BlockSpec((tm, on.shape[1]), row),
                  mod(2), mod(3), mod(4),
                  pl.BlockSpec((None, 1, d), lay3),
                  pl.BlockSpec((None, H_MLA // 2, 2 * KV_LORA, 2 * HEAD_V), lambda i, l: (l[0], 0, 0, 0)),
                  pl.BlockSpec((None,) + wo.shape[1:], lay3),
                  pl.BlockSpec((None, d, LANES), lay3),
                  pl.BlockSpec((None, 1, LANES), lay3)],
        out_specs=[pl.BlockSpec((g8, SUBLANES, d), lambda i, l: (i, 0, 0)),
                   pl.BlockSpec((tm, d), row), pl.BlockSpec((tm, LANES), row), pl.BlockSpec((tm, LANES), row)],
    )
    return pl.pallas_call(
        _post_attn_kernel,
        grid_spec=grid_spec,
        out_shape=[jax.ShapeDtypeStruct((n8, SUBLANES, d), F32), jax.ShapeDtypeStruct((n, d), F32),
                   jax.ShapeDtypeStruct((n, LANES), jnp.int32), jax.ShapeDtypeStruct((n, LANES), F32)],
        compiler_params=_cparams(),
        name="out_proj_router",
    )(lidx, x3, od, ol, on, modg, modg, modg, g_mlp, wuv, wo, wr, br)


def _moe_row_copies(h_hbm, src_ref, blk, xbuf, sem, slot, start):
    def body(r, _):
        tok = src_ref[blk * MOE_ROWS + r]
        cp = pltpu.make_async_copy(h_hbm.at[pl.ds(tok, 1)], xbuf.at[slot, pl.ds(r, 1)], sem.at[slot])
        if start:
            cp.start()
        else:
            cp.wait()
        return 0
    lax.fori_loop(0, MOE_ROWS, body, 0)


def _moe_kernel(l_ref, be_ref, src_ref, h_hbm, wgu_ref, bgu_ref, wd_ref, bd_ref, o_ref, xbuf, sem, wgu_sc, wd_sc,
                *, d_ff):
    i = pl.program_id(0)
    nb = pl.num_programs(0)
    slot = i % 2

    @pl.when(i == 0)
    def _():
        _moe_row_copies(h_hbm, src_ref, 0, xbuf, sem, 0, True)

    @pl.when(i + 1 < nb)
    def _():
        _moe_row_copies(h_hbm, src_ref, i + 1, xbuf, sem, 1 - slot, True)

    @pl.when((i == 0) | (be_ref[i] != be_ref[jnp.maximum(i - 1, 0)]))
    def _():
        wgu_sc[...] = wgu_ref[...].astype(BF16)
        wd_sc[...] = wd_ref[...].astype(BF16)

    _moe_row_copies(h_hbm, src_ref, i, xbuf, sem, slot, False)
    x = xbuf[slot].astype(BF16)
    gu = jnp.dot(x, wgu_sc[...], preferred_element_type=F32) + bgu_ref[...]
    g = jnp.minimum(gu[:, :d_ff], SWIGLU_LIMIT)
    u = jnp.clip(gu[:, d_ff:], -SWIGLU_LIMIT, SWIGLU_LIMIT)
    a = (u + 1.0) * g * _sigmoid(SWIGLU_ALPHA * g)
    o_ref[...] = jnp.dot(a.astype(BF16), wd_sc[...], preferred_element_type=F32) + bd_ref[...]


def _moe_call(h2, w_gate_up, b_gate_up, w_down, b_down, blk_e, src_tok, lidx):
    n, d = h2.shape
    depth, n_exp, _, ff2 = w_gate_up.shape
    d_ff = ff2 // 2
    n_blocks = blk_e.shape[0]
    grid_spec = pltpu.PrefetchScalarGridSpec(
        num_scalar_prefetch=3,
        grid=(n_blocks,),
        in_specs=[pl.BlockSpec(memory_space=pl.ANY),
                  pl.BlockSpec((None, None, d, ff2), lambda i, l, be, src: (l[0], be[i], 0, 0)),
                  pl.BlockSpec((None, None, 1, ff2), lambda i, l, be, src: (l[0], be[i], 0, 0)),
                  pl.BlockSpec((None, None, d_ff, d), lambda i, l, be, src: (l[0], be[i], 0, 0)),
                  pl.BlockSpec((None, None, 1, d), lambda i, l, be, src: (l[0], be[i], 0, 0))],
        out_specs=pl.BlockSpec((MOE_ROWS, d), lambda i, l, be, src: (i, 0)),
        scratch_shapes=[pltpu.VMEM((2, MOE_ROWS, d), F32), pltpu.SemaphoreType.DMA((2,)),
                        pltpu.VMEM((d, ff2), BF16), pltpu.VMEM((d_ff, d), BF16)],
    )
    return pl.pallas_call(
        functools.partial(_moe_kernel, d_ff=d_ff),
        grid_spec=grid_spec,
        out_shape=jax.ShapeDtypeStruct((n_blocks * MOE_ROWS, d), F32),
        compiler_params=_cparams(),
        name="moe_experts",
    )(lidx, blk_e, src_tok, h2, w_gate_up, b_gate_up.reshape(depth, n_exp, 1, ff2), w_down,
      b_down.reshape(depth, n_exp, 1, d))


def _combine_row_copies(yd_hbm, dest_ref, blk, ybuf, sem, slot, tc, start):
    def body(r, _):
        for k in range(TOP_K):
            row = dest_ref[(blk * tc + r) * TOP_K + k]
            cp = pltpu.make_async_copy(yd_hbm.at[pl.ds(row, 1)], ybuf.at[slot, k, pl.ds(r, 1)], sem.at[slot])
            if start:
                cp.start()
            else:
                cp.wait()
        return 0
    lax.fori_loop(0, tc, body, 0)


def _combine_kernel(l_ref, dest_ref, x_ref, ga_ref, gate_ref, yd_hbm, o_ref, ybuf, sem, *, tc):
    i = pl.program_id(0)
    nb = pl.num_programs(0)
    slot = i % 2

    @pl.when(i == 0)
    def _():
        _combine_row_copies(yd_hbm, dest_ref, 0, ybuf, sem, 0, tc, True)

    @pl.when(i + 1 < nb)
    def _():
        _combine_row_copies(yd_hbm, dest_ref, i + 1, ybuf, sem, 1 - slot, tc, True)

    _combine_row_copies(yd_hbm, dest_ref, i, ybuf, sem, slot, tc, False)
    g8, _, d = x_ref.shape
    gate = gate_ref[...]
    acc = ybuf[slot, 0] * gate[:, 0:1]
    for k in range(1, TOP_K):
        acc = acc + ybuf[slot, k] * gate[:, k:k + 1]
    o_ref[...] = x_ref[...] + ga_ref[...] * acc.reshape(g8, SUBLANES, d)


def _combine_call(x3, modg, gate, yd, dest, lidx, tc):
    n8, _, d = x3.shape
    n = n8 * SUBLANES
    g8 = tc // SUBLANES
    grid_spec = pltpu.PrefetchScalarGridSpec(
        num_scalar_prefetch=2,
        grid=(n // tc,),
        in_specs=[pl.BlockSpec((g8, SUBLANES, d), lambda i, l, dst: (i, 0, 0)),
                  pl.BlockSpec((None, None, g8, 1, d), lambda i, l, dst: (l[0], 5, i, 0, 0)),
                  pl.BlockSpec((tc, LANES), lambda i, l, dst: (i, 0)),
                  pl.BlockSpec(memory_space=pl.ANY)],
        out_specs=pl.BlockSpec((g8, SUBLANES, d), lambda i, l, dst: (i, 0, 0)),
        scratch_shapes=[pltpu.VMEM((2, TOP_K, tc, d), F32), pltpu.SemaphoreType.DMA((2,))],
    )
    return pl.pallas_call(
        functools.partial(_combine_kernel, tc=tc),
        grid_spec=grid_spec,
        out_shape=jax.ShapeDtypeStruct((n8, SUBLANES, d), F32),
        compiler_params=_cparams(),
        name="moe_combine",
    )(lidx, dest, x3, modg, gate, yd)


def _final_norm_kernel(x_ref, g_ref, o_ref):
    o_ref[...] = _rms(x_ref[...], g_ref[...])


def _final_norm_call(x, g, tm):
    n, d = x.shape
    return pl.pallas_call(
        _final_norm_kernel,
        grid=(n // tm,),
        in_specs=[pl.BlockSpec((tm, d), lambda i: (i, 0)), pl.BlockSpec((1, d), lambda i: (0, 0))],
        out_specs=pl.BlockSpec((tm, d), lambda i: (i, 0)),
        out_shape=jax.ShapeDtypeStruct((n, d), F32),
        compiler_params=_cparams(),
        name="final_norm",
    )(x, g.reshape(1, d))


def _rel_bucket(dist):
    dist = jnp.maximum(dist, 0)
    exact = N_BUCKETS // 2
    large = exact + (jnp.log(jnp.maximum(dist, 1).astype(F32) / exact)
                     / math.log(MAX_DISTANCE / exact) * (N_BUCKETS - exact)).astype(jnp.int32)
    return jnp.where(dist < exact, dist, jnp.minimum(large, N_BUCKETS - 1))


def _bias_tables(rel_bias, seq, past, seq_new):
    by_dist = rel_bias[_rel_bucket(jnp.arange(LANES, dtype=jnp.int32))].T
    t = jnp.arange(LANES, dtype=jnp.int32)
    idx = jnp.clip(t[None, :, None] - t[None, None, :] + LANES * jnp.arange(3, dtype=jnp.int32)[:, None, None],
                   0, LANES - 1)
    toep = by_dist[:, idx]
    t_diff, t_nsa = toep[:H_DIFF], toep[H_DIFF:]
    stack = lambda tt: tt[:, :, :SUBLANES, :].transpose(1, 0, 2, 3).reshape(3, -1, LANES)
    nsa_dist = by_dist[H_DIFF:]
    n16p = seq // CMP_STRIDE
    cmp_end = jnp.arange(n16p, dtype=jnp.int32) * CMP_STRIDE + CMP_BLOCK - 1
    dist_p = jnp.clip(jnp.arange(seq, dtype=jnp.int32)[:, None] - cmp_end[None, :], 0, LANES - 1)
    bias_cp = nsa_dist[:, dist_p]
    n16s = (past + seq_new) // CMP_STRIDE
    cmp_end = jnp.arange(n16s, dtype=jnp.int32) * CMP_STRIDE + CMP_BLOCK - 1
    dist_s = jnp.clip(past + jnp.arange(seq_new, dtype=jnp.int32)[:, None] - cmp_end[None, :], 0, LANES - 1)
    bias_cs = nsa_dist[:, dist_s].reshape(H_NSA * seq_new, n16s)
    return t_diff, t_nsa, stack(t_diff), stack(t_nsa), bias_cp, bias_cs


def _rope_tables(pos):
    half = D_ROPE // 2
    inv = ROPE_BASE ** (-jnp.arange(half, dtype=F32) / half)
    ang = pos.astype(F32)[:, None] * inv
    cos, sin = jnp.cos(ang), jnp.sin(ang)
    rep = LANES // D_ROPE
    return jnp.tile(jnp.concatenate([cos, cos], axis=1), (1, rep)), jnp.tile(jnp.concatenate([-sin, sin], axis=1), (1, rep))


def _routing(eidx, n_exp):
    n = eidx.shape[0]
    n_assign = n * TOP_K
    flat_e = eidx[:, :TOP_K].reshape(-1)
    flat_t = jnp.arange(n_assign, dtype=jnp.int32) // TOP_K
    order = jnp.argsort(flat_e)
    se, st = flat_e[order], flat_t[order]
    counts = jnp.bincount(flat_e, length=n_exp).astype(jnp.int32)
    padded = (counts + MOE_ROWS - 1) // MOE_ROWS * MOE_ROWS
    pend = jnp.cumsum(padded)
    pstart = pend - padded
    cstart = jnp.cumsum(counts) - counts
    dest_sorted = pstart[se] + jnp.arange(n_assign, dtype=jnp.int32) - cstart[se]
    dest = jnp.zeros((n_assign,), jnp.int32).at[order].set(dest_sorted)
    n_blocks = -(-(n_assign + n_exp * (MOE_ROWS - 1)) // MOE_ROWS)
    blk_e = jnp.minimum(jnp.searchsorted(pend, jnp.arange(n_blocks, dtype=jnp.int32) * MOE_ROWS, side='right'),
                        n_exp - 1).astype(jnp.int32)
    slot = jnp.arange(n_blocks * MOE_ROWS, dtype=jnp.int32)
    e_of = jnp.repeat(blk_e, MOE_ROWS)
    within = slot - pstart[e_of]
    ok = (within >= 0) & (within < counts[e_of])
    src_tok = jnp.where(ok, st[jnp.clip(cstart[e_of] + within, 0, n_assign - 1)], 0).astype(jnp.int32)
    return blk_e, src_tok, dest


def _stack_heads(a, nb, heads, width):
    return a.reshape(nb, SUBLANES, heads, width).transpose(0, 2, 1, 3).reshape(nb, heads * SUBLANES, width)


def _unstack_heads(a, nb, heads):
    width = a.shape[-1]
    return a.reshape(nb, heads, SUBLANES, width).transpose(0, 2, 1, 3).reshape(nb * SUBLANES, heads * width)


def _pad_lanes(a, before, total):
    pad = [(0, 0)] * (a.ndim - 1) + [(before, total - before - a.shape[-1])]
    return jnp.pad(a, pad)


def kernel(x_prompt, x_sample, cache_diff, cache_mla, cache_cmp, cache_slc, state_win, page_table, c_prompt, c_sample,
           w_mod, b_mod, g_attn, g_mlp, w_in, g_q_lat, g_kv_lat, w_uq, w_uk, w_uv, diff_lambda, g_diff_sub, w_cmp,
           w_out, w_router, b_router, w_gate_up, b_gate_up, w_down, b_down, rel_bias, g_final):
    batch, seq, d = x_prompt.shape
    nb, seq_new, _ = x_sample.shape
    assert seq_new == SUBLANES
    depth = w_mod.shape[0]
    n_exp = w_router.shape[2]
    page = cache_diff.shape[2]
    past = page_table.shape[1] * page
    n_p, n_s = batch * seq, nb * seq_new
    n = n_p + n_s
    tm = _pick_tile(n, 256)
    tq, tk = 256, 512
    assert seq % tk == 0 and past % tk == 0 and WINDOW <= tk and past >= WINDOW and seq >= WINDOW
    assert n_p % tm == 0
    page_table = page_table.astype(jnp.int32)

    o = [0]
    for w in (256, 256, 32, 32, 64, 384, 128, 32, 768, 64, 64, 64, 64, 64, 64, 36):
        o.append(o[-1] + w)
    kr0, qn0, gt0 = o[7], o[8], o[15]
    w_in_p = jnp.concatenate(
        [w_in[:, :, :kr0], w_in[:, :, qn0:gt0], w_in[:, :, kr0:qn0],
         w_in[:, :, kr0 + D_ROPE // 2:qn0], w_in[:, :, kr0:kr0 + D_ROPE // 2], w_in[:, :, gt0:],
         jnp.zeros((depth, d, LANES - 2 * D_ROPE - H_NSA * N_BRANCH), F32)], axis=2).astype(BF16)
    dq = D_NOPE + D_ROPE
    wq_nope = w_uq[..., :D_NOPE].reshape(depth, Q_LORA, H_MLA * D_NOPE).astype(BF16)
    wq_rope = w_uq[..., D_NOPE:].reshape(depth, Q_LORA, H_MLA * D_ROPE).astype(BF16)
    wq_rope_sw = jnp.concatenate([w_uq[..., D_NOPE + D_ROPE // 2:], w_uq[..., D_NOPE:D_NOPE + D_ROPE // 2]],
                                 axis=-1).reshape(depth, Q_LORA, H_MLA * D_ROPE).astype(BF16)
    del dq
    w_ukt = w_uk.transpose(0, 2, 3, 1).astype(BF16)
    w_uvh = w_uv.transpose(0, 2, 1, 3).reshape(depth, H_MLA // 2, 2, KV_LORA, HEAD_V)
    zero = jnp.zeros_like(w_uvh[:, :, 0])
    w_uvh = jnp.concatenate([jnp.concatenate([w_uvh[:, :, 0], zero], axis=3),
                             jnp.concatenate([zero, w_uvh[:, :, 1]], axis=3)], axis=2).astype(BF16)
    w_out_b = w_out.astype(BF16)
    w_r = _pad_lanes(w_router, 0, LANES).astype(BF16)
    b_r = jnp.concatenate([b_router, jnp.full((depth, LANES - n_exp), NEG, F32)], axis=1).reshape(depth, 1, LANES)
    g_diff128 = jnp.tile(g_diff_sub, (1, LANES // HEAD_V)).reshape(depth, 1, LANES)
    lam_init = jnp.asarray([0.8 - 0.6 * math.exp(-0.3 * l) for l in range(depth)], F32).reshape(depth, 1, 1)
    w_lane = jnp.concatenate([jnp.broadcast_to(w_cmp[:, 0, :, None], (depth, CMP_BLOCK, DK_NSA)),
                              jnp.broadcast_to(w_cmp[:, 1, :, None], (depth, CMP_BLOCK, HEAD_V))], axis=2)
    t_diff, t_nsa, ts_diff, ts_nsa, bias_cp, bias_cs = _bias_tables(rel_bias, seq, past, seq_new)
    pos = jnp.concatenate([jnp.tile(jnp.arange(seq, dtype=jnp.int32), batch),
                           jnp.tile(past + jnp.arange(seq_new, dtype=jnp.int32), nb)])
    rope_c, rope_s = _rope_tables(pos)
    n_sel_p = -(-(-(-seq // SEL_BLOCK)) // LANES) * LANES
    n_sel_s = -(-(-(-(past + seq_new) // SEL_BLOCK)) // LANES) * LANES

    m_rows = -(-(batch + nb) // SUBLANES) * SUBLANES
    c_all = jnp.pad(jnp.concatenate([c_prompt, c_sample], axis=0), ((0, m_rows - batch - nb), (0, 0)))
    mod = _mod_call(c_all, w_mod, b_mod).reshape(depth, m_rows, 6, d)
    modg = jnp.concatenate([jnp.repeat(mod[:, :batch], seq // SUBLANES, axis=1), mod[:, batch:batch + nb]], axis=1)
    modg = modg.transpose(0, 2, 1, 3).reshape(depth, 6, n // SUBLANES, 1, d)

    x3 = jnp.concatenate([x_prompt.reshape(n_p, d), x_sample.reshape(n_s, d)], axis=0).reshape(n // SUBLANES, SUBLANES, d)
    g_attn3, g_mlp3 = g_attn.reshape(depth, 1, d), g_mlp.reshape(depth, 1, d)
    g_q3, g_kv3 = g_q_lat.reshape(depth, 1, Q_LORA), g_kv_lat.reshape(depth, 1, KV_LORA)

    rows_out = {k: [] for k in ("diff", "mla", "cmp", "slc", "win")}
    for l in range(depth):
        lidx = jnp.full((1,), l, jnp.int32)
        qd, rdiff, cq, ckv, qn, rcmp, rslc, rwin, misc = _proj_call(x3, modg, g_attn3, w_in_p, lidx, tm)
        qabs, rmla = _mla_prep_call(cq, ckv, misc, rope_c, rope_s, g_q3, g_kv3, wq_nope, wq_rope, wq_rope_sw, w_ukt,
                                    lidx, tm)
        for k, r in (("diff", rdiff), ("mla", rmla), ("cmp", rcmp), ("slc", rslc), ("win", rwin)):
            rows_out[k].append(r)

        od_p = _diff_prompt_call(qd, rdiff, t_diff, diff_lambda, lam_init, g_diff128, lidx, batch, seq, tq, tk)
        ol_p = _mla_prompt_call(qabs, rmla, batch, seq, tq, tk)
        cmp_p = _compress_prompt_call(rcmp, w_lane, lidx, batch, seq)
        oc_p, sel_p = _cmp_prompt_call(qn, cmp_p, bias_cp, batch, seq, tq, n_sel_p)
        on_p = _nsa_prompt_call(qn, rslc, rwin, t_nsa, sel_p, oc_p, misc, batch, seq, tq, tk)

        hq = H_DIFF * DK_DIFF
        q1s = _pad_lanes(_stack_heads(qd[n_p:, :hq], nb, H_DIFF, DK_DIFF), 0, LANES)
        q2s = _pad_lanes(_stack_heads(qd[n_p:, hq:], nb, H_DIFF, DK_DIFF), DK_DIFF, LANES)
        od_s = _diff_sample_call(jnp.concatenate([q1s, q2s], axis=1), rdiff, ts_diff, diff_lambda, lam_init, g_diff128,
                                 cache_diff, page_table, lidx, n_p, past, tk)
        od_s = _unstack_heads(od_s[..., HEAD_V:], nb, H_DIFF)
        ol_s = _mla_sample_call(_stack_heads(qabs[n_p:], nb, H_MLA, MLA_W), rmla, cache_mla, page_table, lidx, n_p,
                                past, tk)
        ol_s = _unstack_heads(ol_s, nb, H_MLA)
        qn_s = _pad_lanes(_stack_heads(qn[n_p:], nb, H_NSA, DK_NSA), 0, LANES)
        cmp_s = _compress_sample_call(w_lane, cache_cmp, page_table, lidx, nb, past)
        oc_s, sel_s = _cmp_sample_call(qn_s, cmp_s, bias_cs, past, seq_new, n_sel_s)
        gt_s = _pad_lanes(_stack_heads(misc[n_p:, 2 * D_ROPE:2 * D_ROPE + H_NSA * N_BRANCH], nb, H_NSA, N_BRANCH),
                          0, LANES)
        on_s = _nsa_sample_call(qn_s, rslc, rwin, state_win, ts_nsa, jnp.tile(sel_s, (1, H_NSA, 1)), oc_s, gt_s,
                                cache_slc, page_table, lidx, n_p, past, tk)
        on_s = _unstack_heads(on_s[..., HEAD_V:], nb, H_NSA)

        od = jnp.concatenate([od_p, od_s], axis=0)
        ol = jnp.concatenate([ol_p, ol_s], axis=0)
        on = jnp.concatenate([on_p, on_s], axis=0)
        x3, h2, eidx, gate = _post_attn_call(x3, od, ol, on, modg, g_mlp3, w_uvh, w_out_b, w_r, b_r, lidx, tm)
        blk_e, src_tok, dest = _routing(eidx, n_exp)
        yd = _moe_call(h2, w_gate_up, b_gate_up, w_down, b_down, blk_e, src_tok, lidx)
        x3 = _combine_call(x3, modg, gate, yd, dest, lidx, _pick_tile(n, 64))

    y = _final_norm_call(x3.reshape(n, d), g_final, tm)
    y_prompt = y[:n_p].reshape(batch, seq, d)
    y_sample = y[n_p:].reshape(nb, seq_new, d)
    stk = lambda k, sl, shape: jnp.stack([r[sl] for r in rows_out[k]]).reshape((depth,) + shape)
    outs_p = [stk(k, slice(0, n_p), (batch, seq, -1)) for k in ("diff", "mla", "cmp", "slc", "win")]
    outs_s = [stk(k, slice(n_p, n), (nb, seq_new, -1)) for k in ("diff", "mla", "cmp", "slc", "win")]
    keep_p = min(WINDOW, seq)
    win_p = outs_p[4][:, :, seq - keep_p:]
    keep_s = state_win.shape[2]
    win_s = jnp.concatenate([state_win, outs_s[4]], axis=2)[:, :, seq_new:seq_new + keep_s]
    return (y_prompt, y_sample, outs_p[0], outs_p[1], outs_p[2], outs_p[3], win_p,
            outs_s[0], outs_s[1], outs_s[2], outs_s[3], win_s)
```

```python
import functools
import math

import jax
import jax.numpy as jnp
from jax import lax
from jax.experimental import pallas as pl
from jax.experimental.pallas import tpu as pltpu

F32 = jnp.float32
BF16 = jnp.bfloat16
NEG = -1e30

HEAD_V = 64
H_DIFF = 8
DK_DIFF = 32
H_MLA = 12
Q_LORA = 384
KV_LORA = 128
D_NOPE = 64
D_ROPE = 32
H_NSA = 12
DK_NSA = 64
CMP_BLOCK = 32
CMP_STRIDE = 16
SEL_BLOCK = 64
N_SEL = 16
WINDOW = 512
N_BRANCH = 3
TOP_K = 4
SWIGLU_LIMIT = 7.0
SWIGLU_ALPHA = 1.702
N_BUCKETS = 32
MAX_DISTANCE = 128
ROPE_BASE = 10000.0
EPS = 1e-6
FORCE_SCORE = 1e9

LANES = 128
SUBLANES = 8
MOE_ROWS = 256
SAMPLE_CHAINS = 4
HEAD_GROUP = 4
VMEM_LIMIT = 56 * 1024 * 1024

PROJ_SPLITS = (2 * H_DIFF * DK_DIFF, 128, Q_LORA, KV_LORA, H_NSA * DK_NSA, 128, 128, 128, 128)


def _cparams(**kw):
    kw.setdefault("vmem_limit_bytes", VMEM_LIMIT)
    return pltpu.CompilerParams(**kw)


def _pick_tile(n, pref):
    t = pref
    while n % t:
        t //= 2
    return t


def _n_chain(past, tk):
    return math.gcd(SAMPLE_CHAINS, past // tk)


def _rms(x, g):
    return x * lax.rsqrt(jnp.mean(x * x, axis=-1, keepdims=True) + EPS) * g


def _sigmoid(x):
    return 1.0 / (1.0 + jnp.exp(-x))


def _iota(shape, dim):
    return lax.broadcasted_iota(jnp.int32, shape, dim)


def _mod_kernel(c_ref, w_ref, b_ref, o_ref):
    c = c_ref[...]
    a = (c * _sigmoid(c)).astype(BF16)
    o_ref[...] = jnp.dot(a, w_ref[...].astype(BF16), preferred_element_type=F32) + b_ref[...]


def _mod_call(c_all, w_mod, b_mod):
    depth, d, n6 = w_mod.shape
    m = c_all.shape[0]
    tn = _pick_tile(n6, 1024)
    return pl.pallas_call(
        _mod_kernel,
        grid=(depth, n6 // tn),
        in_specs=[pl.BlockSpec((m, d), lambda l, j: (0, 0)),
                  pl.BlockSpec((None, d, tn), lambda l, j: (l, 0, j)),
                  pl.BlockSpec((None, 1, tn), lambda l, j: (l, 0, j))],
        out_specs=pl.BlockSpec((None, m, tn), lambda l, j: (l, 0, j)),
        out_shape=jax.ShapeDtypeStruct((depth, m, n6), F32),
        compiler_params=_cparams(),
        name="adaln_mod",
    )(c_all, w_mod, b_mod.reshape(depth, 1, n6))


def _proj_kernel(l_ref, x_ref, sh_ref, sc_ref, g_ref, w_ref, *outs):
    g8, _, d = x_ref.shape
    y = _rms(x_ref[...], g_ref[...])
    h = y * (1.0 + sc_ref[...]) + sh_ref[...]
    h = h.reshape(g8 * SUBLANES, d).astype(BF16)
    r = jnp.dot(h, w_ref[...], preferred_element_type=F32)
    off = 0
    for o in outs:
        wdt = o.shape[1]
        o[...] = r[:, off:off + wdt]
        off += wdt


def _proj_call(x3, modg, g_attn, w_in_p, lidx, tm):
    n8, _, d = x3.shape
    n = n8 * SUBLANES
    g8 = tm // SUBLANES
    p = w_in_p.shape[2]
    grid_spec = pltpu.PrefetchScalarGridSpec(
        num_scalar_prefetch=1,
        grid=(n // tm,),
        in_specs=[pl.BlockSpec((g8, SUBLANES, d), lambda i, l: (i, 0, 0)),
                  pl.BlockSpec((None, None, g8, 1, d), lambda i, l: (l[0], 0, i, 0, 0)),
                  pl.BlockSpec((None, None, g8, 1, d), lambda i, l: (l[0], 1, i, 0, 0)),
                  pl.BlockSpec((None, 1, d), lambda i, l: (l[0], 0, 0)),
                  pl.BlockSpec((None, d, p), lambda i, l: (l[0], 0, 0))],
        out_specs=[pl.BlockSpec((tm, w), lambda i, l: (i, 0)) for w in PROJ_SPLITS],
    )
    return pl.pallas_call(
        _proj_kernel,
        grid_spec=grid_spec,
        out_shape=[jax.ShapeDtypeStruct((n, w), F32) for w in PROJ_SPLITS],
        compiler_params=_cparams(),
        name="in_proj",
    )(lidx, x3, modg, modg, g_attn, w_in_p)


def _mla_prep_kernel(l_ref, cq_ref, ckv_ref, misc_ref, cc_ref, ss_ref, gq_ref, gkv_ref, wn_ref, wr_ref, wrs_ref,
                     wuk_ref, qabs_ref, row_ref):
    tm = cq_ref.shape[0]
    cqn = _rms(cq_ref[...], gq_ref[...]).astype(BF16)
    qn = jnp.dot(cqn, wn_ref[...], preferred_element_type=F32)
    qr = jnp.dot(cqn, wr_ref[...], preferred_element_type=F32)
    qrs = jnp.dot(cqn, wrs_ref[...], preferred_element_type=F32)
    cc = cc_ref[...]
    ss = ss_ref[...]
    per = LANES // D_ROPE
    roped = [qr[:, LANES * k:LANES * (k + 1)] * cc + qrs[:, LANES * k:LANES * (k + 1)] * ss
             for k in range(H_MLA // per)]
    wq = KV_LORA + LANES
    for h in range(H_MLA):
        ql = jnp.dot(qn[:, D_NOPE * h:D_NOPE * (h + 1)].astype(BF16), wuk_ref[h], preferred_element_type=F32)
        qabs_ref[:, wq * h:wq * h + KV_LORA] = ql
        qabs_ref[:, wq * h + KV_LORA:wq * (h + 1)] = _lane_pick(roped[h // per], D_ROPE * (h % per), D_ROPE, 0)
    misc = misc_ref[...]
    krr = misc * cc + pltpu.roll(misc, LANES - D_ROPE, 1) * ss
    row_ref[:, 0:KV_LORA] = _rms(ckv_ref[...], gkv_ref[...])
    row_ref[:, KV_LORA:KV_LORA + D_ROPE] = krr[:, 0:D_ROPE]


def _mla_prep_call(cq, ckv, misc, cc, ss, g_q, g_kv, wn, wr, wrs, wuk, lidx, tm):
    n = cq.shape[0]
    wq = KV_LORA + LANES
    row = lambda i, l: (i, 0)
    lay3 = lambda i, l: (l[0], 0, 0)
    grid_spec = pltpu.PrefetchScalarGridSpec(
        num_scalar_prefetch=1,
        grid=(n // tm,),
        in_specs=[pl.BlockSpec((tm, Q_LORA), row), pl.BlockSpec((tm, KV_LORA), row),
                  pl.BlockSpec((tm, LANES), row), pl.BlockSpec((tm, LANES), row), pl.BlockSpec((tm, LANES), row),
                  pl.BlockSpec((None, 1, Q_LORA), lay3), pl.BlockSpec((None, 1, KV_LORA), lay3),
                  pl.BlockSpec((None, Q_LORA, H_MLA * D_NOPE), lay3),
                  pl.BlockSpec((None, Q_LORA, H_MLA * D_ROPE), lay3),
                  pl.BlockSpec((None, Q_LORA, H_MLA * D_ROPE), lay3),
                  pl.BlockSpec((None, H_MLA, D_NOPE, KV_LORA), lambda i, l: (l[0], 0, 0, 0))],
        out_specs=[pl.BlockSpec((tm, H_MLA * wq), row), pl.BlockSpec((tm, KV_LORA + D_ROPE), row)],
    )
    return pl.pallas_call(
        _mla_prep_kernel,
        grid_spec=grid_spec,
        out_shape=[jax.ShapeDtypeStruct((n, H_MLA * wq), F32), jax.ShapeDtypeStruct((n, KV_LORA + D_ROPE), F32)],
        compiler_params=_cparams(),
        name="mla_prep",
    )(lidx, cq, ckv, misc, cc, ss, g_q, g_kv, wn, wr, wrs, wuk)


def _flash_init(n_maps, rows, width):
    return tuple((jnp.full((rows, 1), NEG, F32), jnp.zeros((rows, 1), F32), jnp.zeros((rows, width), F32))
                 for _ in range(n_maps))


_NT = (((1,), (1,)), ((), ()))


def _flash_one(carry, q, kk, vv, scale, bias=None, add=None, keep=None, keys_t=False):
    m, l, acc = carry
    if keys_t:
        s = jnp.dot(q, kk, preferred_element_type=F32) * scale
    else:
        s = lax.dot_general(q, kk, _NT, preferred_element_type=F32) * scale
    if bias is not None:
        s = s + bias
    if add is not None:
        s = s + add
    if keep is not None:
        s = jnp.where(keep, s, NEG)
    m_new = jnp.maximum(m, jnp.max(s, axis=1, keepdims=True))
    alpha = jnp.exp(m - m_new)
    p = jnp.exp(s - m_new)
    l = alpha * l + jnp.sum(p, axis=1, keepdims=True)
    if keys_t:
        pv = lax.dot_general(p.astype(BF16), vv, _NT, preferred_element_type=F32)
    else:
        pv = jnp.dot(p.astype(BF16), vv, preferred_element_type=F32)
    return m_new, l, alpha * acc + pv


def _flash_merge(carry):
    m = carry[0][0]
    for mj, _, _ in carry[1:]:
        m = jnp.maximum(m, mj)
    l = acc = None
    for mj, lj, aj in carry:
        w = jnp.exp(mj - m)
        l = w * lj if l is None else l + w * lj
        acc = w * aj if acc is None else acc + w * aj
    return l, acc


def _bias_prompt(t_ref, h, blk0, tq, tk):
    rows = []
    for a in range(tq // LANES):
        cols = [t_ref[h, jnp.clip(blk0 + a - b, 0, 2)] for b in range(tk // LANES)]
        rows.append(jnp.concatenate(cols, axis=1) if len(cols) > 1 else cols[0])
    return jnp.concatenate(rows, axis=0) if len(rows) > 1 else rows[0]


def _bias_stacked(ts_ref, blk0, tk):
    cols = [ts_ref[jnp.clip(blk0 - b, 0, 2)] for b in range(tk // LANES)]
    return jnp.concatenate(cols, axis=1) if len(cols) > 1 else cols[0]


def _lane_pick(x, src, width, dst):
    shift = (dst - src) % LANES
    y = pltpu.roll(x, shift, 1) if shift else x
    lane = _iota(x.shape, 1)
    return jnp.where((lane >= dst) & (lane < dst + width), y, 0.0)


def _rows_dst(buf, page):
    return lambda slot, p: buf.at[slot, pl.ds(p * page, page)]


def _cols_dst(buf, page, tk):
    per = tk // page
    return lambda slot, p: buf.at[slot, p // per, :, pl.ds((p % per) * page, page)]


def _page_copies(pt_ref, cache_hbm, l, b, dst, sem, slot, n_pages):
    return [pltpu.make_async_copy(cache_hbm.at[l, pt_ref[b, p]], dst(slot, p), sem.at[slot])
            for p in range(n_pages)]


def _paged_fetch(pt_ref, cache_hbm, l, dst, sem, n_pages):
    b = pl.program_id(0)
    nb = pl.num_programs(0)
    slot = b % 2

    @pl.when(b == 0)
    def _():
        for cp in _page_copies(pt_ref, cache_hbm, l, 0, dst, sem, 0, n_pages):
            cp.start()

    @pl.when(b + 1 < nb)
    def _():
        for cp in _page_copies(pt_ref, cache_hbm, l, b + 1, dst, sem, 1 - slot, n_pages):
            cp.start()

    for cp in _page_copies(pt_ref, cache_hbm, l, b, dst, sem, slot, n_pages):
        cp.wait()
    return slot


def _tail_rows(new_rows, width):
    t = new_rows
    if t.shape[1] < width:
        t = jnp.concatenate([t, jnp.zeros((t.shape[0], width - t.shape[1]), F32)], axis=1)
    return jnp.concatenate([t, jnp.zeros((LANES - t.shape[0], width), F32)], axis=0).astype(BF16)


def _diff_lambda(lq_ref, li_ref):
    lq = lq_ref[...]
    a = jnp.exp(jnp.sum(lq[0:1] * lq[1:2], axis=1, keepdims=True))
    b = jnp.exp(jnp.sum(lq[2:3] * lq[3:4], axis=1, keepdims=True))
    return a - b + li_ref[...]


def _diff_finish(o1, o2, lam, li, g128):
    o = o1 - lam * o2
    lane = _iota(o.shape, 1)
    ms = jnp.sum(jnp.where(lane >= HEAD_V, o * o, 0.0), axis=1, keepdims=True) / HEAD_V
    return o * lax.rsqrt(ms + EPS) * g128 * (1.0 - li)


def _diff_prompt_kernel(l_ref, q_ref, kv_ref, t_ref, lq_ref, li_ref, g_ref, o_ref, kv_sc, *, tq, tk):
    i = pl.program_id(1)

    @pl.when(i == 0)
    def _():
        kv_sc[...] = kv_ref[...].astype(BF16)

    scale = DK_DIFF ** -0.5
    lam = _diff_lambda(lq_ref, li_ref)
    li = li_ref[...]
    g128 = g_ref[...]
    q0 = i * tq
    cd = q0 // tk
    qpos = q0 + _iota((tq, 1), 0)
    keep_d = qpos >= cd * tk + _iota((1, tk), 1)
    per = LANES // DK_DIFF
    hq = H_DIFF * DK_DIFF
    lane = _iota((tq, LANES), 1)
    for hp in range(H_DIFF // 2):
        qs = []
        for h in (2 * hp, 2 * hp + 1):
            blk, off = h // per, DK_DIFF * (h % per)
            qs.append(_lane_pick(q_ref[:, LANES * blk:LANES * (blk + 1)], off, DK_DIFF, 0).astype(BF16))
            qs.append(_lane_pick(q_ref[:, hq + LANES * blk:hq + LANES * (blk + 1)], off, DK_DIFF, DK_DIFF)
                      .astype(BF16))

        def step(c, carry, keep, qs=qs, hp=hp):
            kk = kv_sc[pl.ds(pl.multiple_of(c * tk, tk), tk), :]
            out = []
            for j in range(2):
                bias = _bias_prompt(t_ref, 2 * hp + j, (q0 - c * tk) // LANES, tq, tk)
                for mi in range(2):
                    out.append(_flash_one(carry[2 * j + mi], qs[2 * j + mi], kk, kk, scale, bias=bias, keep=keep))
            return tuple(out)

        carry = lax.fori_loop(0, cd, lambda c, carry, step=step: step(c, carry, None), _flash_init(4, tq, LANES))
        carry = step(cd, carry, keep_d)
        o = [_diff_finish(carry[2 * j][2] / carry[2 * j][1], carry[2 * j + 1][2] / carry[2 * j + 1][1], lam, li, g128)
             for j in range(2)]
        o_ref[:, LANES * hp:LANES * (hp + 1)] = jnp.where(lane < HEAD_V, pltpu.roll(o[0], HEAD_V, 1), o[1])


def _diff_prompt_call(qd, rdiff, t_diff, lq, li, g128, lidx, batch, seq, tq, tk):
    nq = seq // tq
    grid_spec = pltpu.PrefetchScalarGridSpec(
        num_scalar_prefetch=1,
        grid=(batch, nq),
        in_specs=[pl.BlockSpec((tq, 2 * H_DIFF * DK_DIFF), lambda b, i, l: (b * nq + i, 0)),
                  pl.BlockSpec((seq, LANES), lambda b, i, l: (b, 0)),
                  pl.BlockSpec(t_diff.shape, lambda b, i, l: (0, 0, 0, 0)),
                  pl.BlockSpec((None, 4, DK_DIFF), lambda b, i, l: (l[0], 0, 0)),
                  pl.BlockSpec((None, 1, 1), lambda b, i, l: (l[0], 0, 0)),
                  pl.BlockSpec((None, 1, LANES), lambda b, i, l: (l[0], 0, 0))],
        out_specs=pl.BlockSpec((tq, H_DIFF * HEAD_V), lambda b, i, l: (b * nq + i, 0)),
        scratch_shapes=[pltpu.VMEM((seq, LANES), BF16)],
    )
    return pl.pallas_call(
        functools.partial(_diff_prompt_kernel, tq=tq, tk=tk),
        grid_spec=grid_spec,
        out_shape=jax.ShapeDtypeStruct((batch * seq, H_DIFF * HEAD_V), F32),
        compiler_params=_cparams(),
        name="diff_attn_prompt",
    )(lidx, qd, rdiff, t_diff, lq, li, g128)


def _diff_sample_kernel(pt_ref, l_ref, q_ref, new_ref, ts_ref, lq_ref, li_ref, g_ref, cache_hbm, o_ref, buf, sem,
                        *, past, page, tk, n_chain):
    n_pages = past // page
    slot = _paged_fetch(pt_ref, cache_hbm, l_ref[0], _rows_dst(buf, page), sem, n_pages)
    rows = H_DIFF * SUBLANES
    scale = DK_DIFF ** -0.5
    lam = _diff_lambda(lq_ref, li_ref)
    q = q_ref[...].astype(BF16)
    per = past // tk // n_chain

    def body(c, carry):
        out = []
        for j in range(n_chain):
            cc = c + j * per
            kk = buf[slot, pl.ds(pl.multiple_of(cc * tk, tk), tk), :].astype(BF16)
            b1 = _bias_stacked(ts_ref, (past - cc * tk) // LANES, tk)
            out.append(_flash_one(carry[j], q, kk, kk, scale, bias=jnp.concatenate([b1, b1], axis=0)))
        return tuple(out)

    carry = lax.fori_loop(0, per, body, _flash_init(n_chain, 2 * rows, LANES))
    kt = _tail_rows(new_ref[...], LANES)
    keep = _iota((1, LANES), 1) <= (_iota((2 * rows, 1), 0) % SUBLANES)
    b0 = ts_ref[0]
    tail = _flash_one(carry[0], q, kt, kt, scale, bias=jnp.concatenate([b0, b0], axis=0), keep=keep)
    l, acc = _flash_merge((tail,) + carry[1:])
    o = acc / l
    o_ref[...] = _diff_finish(o[0:rows], o[rows:2 * rows], lam, li_ref[...], g_ref[...])


def _diff_sample_call(q_st, rdiff, ts_diff, lq, li, g128, cache, page_table, lidx, row0, past, tk):
    nb = q_st.shape[0]
    page = cache.shape[2]
    rows = H_DIFF * SUBLANES
    grid_spec = pltpu.PrefetchScalarGridSpec(
        num_scalar_prefetch=2,
        grid=(nb,),
        in_specs=[pl.BlockSpec((None, 2 * rows, LANES), lambda b, pt, l: (b, 0, 0)),
                  pl.BlockSpec((SUBLANES, LANES), lambda b, pt, l: (row0 // SUBLANES + b, 0)),
                  pl.BlockSpec(ts_diff.shape, lambda b, pt, l: (0, 0, 0)),
                  pl.BlockSpec((None, 4, DK_DIFF), lambda b, pt, l: (l[0], 0, 0)),
                  pl.BlockSpec((None, 1, 1), lambda b, pt, l: (l[0], 0, 0)),
                  pl.BlockSpec((None, 1, LANES), lambda b, pt, l: (l[0], 0, 0)),
                  pl.BlockSpec(memory_space=pl.ANY)],
        out_specs=pl.BlockSpec((None, rows, LANES), lambda b, pt, l: (b, 0, 0)),
        scratch_shapes=[pltpu.VMEM((2, past, LANES), F32), pltpu.SemaphoreType.DMA((2,))],
    )
    return pl.pallas_call(
        functools.partial(_diff_sample_kernel, past=past, page=page, tk=tk, n_chain=_n_chain(past, tk)),
        grid_spec=grid_spec,
        out_shape=jax.ShapeDtypeStruct((nb, rows, LANES), F32),
        compiler_params=_cparams(),
        name="diff_attn_sample",
    )(page_table, lidx, q_st, rdiff, ts_diff, lq, li, g128, cache)


MLA_W = KV_LORA + LANES


def _mla_prompt_kernel(q_ref, kv_ref, o_ref, kv_sc, *, tq, tk):
    i = pl.program_id(1)

    @pl.when(i == 0)
    def _():
        kv_sc[...] = jnp.zeros(kv_sc.shape, BF16)
        kv_sc[:, 0:KV_LORA + D_ROPE] = kv_ref[...].astype(BF16)

    scale = (D_NOPE + D_ROPE) ** -0.5
    q0 = i * tq
    cd = q0 // tk
    keep_d = (q0 + _iota((tq, 1), 0)) >= cd * tk + _iota((1, tk), 1)
    for g in range(H_MLA // HEAD_GROUP):
        heads = range(HEAD_GROUP * g, HEAD_GROUP * (g + 1))
        qs = [q_ref[:, MLA_W * h:MLA_W * (h + 1)].astype(BF16) for h in heads]

        def step(c, carry, keep, qs=qs):
            kk = kv_sc[pl.ds(pl.multiple_of(c * tk, tk), tk), :]
            return tuple(_flash_one(cr, q, kk, kk[:, 0:KV_LORA], scale, keep=keep) for cr, q in zip(carry, qs))

        carry = lax.fori_loop(0, cd, lambda c, carry, step=step: step(c, carry, None),
                              _flash_init(HEAD_GROUP, tq, KV_LORA))
        carry = step(cd, carry, keep_d)
        for h, (_, l, acc) in zip(heads, carry):
            o_ref[:, KV_LORA * h:KV_LORA * (h + 1)] = acc / l


def _mla_prompt_call(qabs, rmla, batch, seq, tq, tk):
    nq = seq // tq
    return pl.pallas_call(
        functools.partial(_mla_prompt_kernel, tq=tq, tk=tk),
        grid=(batch, nq),
        in_specs=[pl.BlockSpec((tq, H_MLA * MLA_W), lambda b, i: (b * nq + i, 0)),
                  pl.BlockSpec((seq, KV_LORA + D_ROPE), lambda b, i: (b, 0))],
        out_specs=pl.BlockSpec((tq, H_MLA * KV_LORA), lambda b, i: (b * nq + i, 0)),
        out_shape=jax.ShapeDtypeStruct((batch * seq, H_MLA * KV_LORA), F32),
        scratch_shapes=[pltpu.VMEM((seq, MLA_W), BF16)],
        compiler_params=_cparams(),
        name="mla_attn_prompt",
    )(qabs, rmla)


def _mla_sample_kernel(pt_ref, l_ref, q_ref, new_ref, cache_hbm, o_ref, buf, sem, *, past, page, tk, n_chain):
    n_pages = past // page
    row_w = KV_LORA + D_ROPE
    slot = _paged_fetch(pt_ref, cache_hbm, l_ref[0], _cols_dst(buf, page, tk), sem, n_pages)
    rows = H_MLA * SUBLANES
    scale = (D_NOPE + D_ROPE) ** -0.5
    q = q_ref[:, 0:row_w].astype(BF16)
    per = past // tk // n_chain

    def body(c, carry):
        out = []
        for j in range(n_chain):
            kt = buf[slot, c + j * per].astype(BF16)
            out.append(_flash_one(carry[j], q, kt, kt[0:KV_LORA, :], scale, keys_t=True))
        return tuple(out)

    carry = lax.fori_loop(0, per, body, _flash_init(n_chain, rows, KV_LORA))
    kn = _tail_rows(new_ref[...], row_w)
    keep = _iota((1, LANES), 1) <= (_iota((rows, 1), 0) % SUBLANES)
    tail = _flash_one(carry[0], q, kn, kn[:, 0:KV_LORA], scale, keep=keep)
    l, acc = _flash_merge((tail,) + carry[1:])
    o_ref[...] = acc / l


def _mla_sample_call(q_st, rmla, cache, page_table, lidx, row0, past, tk):
    nb = q_st.shape[0]
    page = cache.shape[3]
    rows = H_MLA * SUBLANES
    grid_spec = pltpu.PrefetchScalarGridSpec(
        num_scalar_prefetch=2,
        grid=(nb,),
        in_specs=[pl.BlockSpec((None, rows, MLA_W), lambda b, pt, l: (b, 0, 0)),
                  pl.BlockSpec((SUBLANES, KV_LORA + D_ROPE), lambda b, pt, l: (row0 // SUBLANES + b, 0)),
                  pl.BlockSpec(memory_space=pl.ANY)],
        out_specs=pl.BlockSpec((None, rows, KV_LORA), lambda b, pt, l: (b, 0, 0)),
        scratch_shapes=[pltpu.VMEM((2, past // tk, KV_LORA + D_ROPE, tk), F32), pltpu.SemaphoreType.DMA((2,))],
    )
    return pl.pallas_call(
        functools.partial(_mla_sample_kernel, past=past, page=page, tk=tk, n_chain=_n_chain(past, tk)),
        grid_spec=grid_spec,
        out_shape=jax.ShapeDtypeStruct((nb, rows, KV_LORA), F32),
        compiler_params=_cparams(),
        name="mla_attn_sample",
    )(page_table, lidx, q_st, rmla, cache)


def _compress(load_rows, w_ref, s1_sc, n16):
    s0 = jnp.zeros((n16, LANES), F32)
    s1 = jnp.zeros((n16, LANES), F32)
    for j in range(CMP_STRIDE):
        x = load_rows(j)
        s0 = s0 + x * w_ref[j:j + 1, :]
        s1 = s1 + x * w_ref[CMP_STRIDE + j:CMP_STRIDE + j + 1, :]
    s1_sc[0:n16, :] = s1
    s1_sc[n16:n16 + SUBLANES, :] = jnp.zeros((SUBLANES, LANES), F32)
    out = s0 + s1_sc[1:n16 + 1, :]
    return jnp.where(_iota((n16, 1), 0) < n16 - 1, out, 0.0)


def _compress_prompt_kernel(l_ref, kv_ref, w_ref, o_ref, s1_sc, *, n16):
    o_ref[...] = _compress(lambda j: kv_ref[pl.ds(j, n16, stride=CMP_STRIDE), :], w_ref, s1_sc, n16)


def _compress_prompt_call(rcmp, w_lane, lidx, batch, seq):
    n16 = seq // CMP_STRIDE
    grid_spec = pltpu.PrefetchScalarGridSpec(
        num_scalar_prefetch=1,
        grid=(batch,),
        in_specs=[pl.BlockSpec((seq, LANES), lambda b, l: (b, 0)),
                  pl.BlockSpec((None, CMP_BLOCK, LANES), lambda b, l: (l[0], 0, 0))],
        out_specs=pl.BlockSpec((None, n16, LANES), lambda b, l: (b, 0, 0)),
        scratch_shapes=[pltpu.VMEM((n16 + SUBLANES, LANES), F32)],
    )
    return pl.pallas_call(
        functools.partial(_compress_prompt_kernel, n16=n16),
        grid_spec=grid_spec,
        out_shape=jax.ShapeDtypeStruct((batch, n16, LANES), F32),
        compiler_params=_cparams(),
        name="nsa_compress_prompt",
    )(lidx, rcmp, w_lane)


def _compress_sample_kernel(pt_ref, l_ref, w_ref, cache_hbm, o_ref, buf, sem, s1_sc, *, past, page):
    slot = _paged_fetch(pt_ref, cache_hbm, l_ref[0], _rows_dst(buf, page), sem, past // page)
    n16 = past // CMP_STRIDE
    o_ref[...] = _compress(lambda j: buf[slot, pl.ds(j, n16, stride=CMP_STRIDE), :], w_ref, s1_sc, n16)


def _compress_sample_call(w_lane, cache, page_table, lidx, nb, past):
    page = cache.shape[2]
    n16 = past // CMP_STRIDE
    grid_spec = pltpu.PrefetchScalarGridSpec(
        num_scalar_prefetch=2,
        grid=(nb,),
        in_specs=[pl.BlockSpec((None, CMP_BLOCK, LANES), lambda b, pt, l: (l[0], 0, 0)),
                  pl.BlockSpec(memory_space=pl.ANY)],
        out_specs=pl.BlockSpec((None, n16, LANES), lambda b, pt, l: (b, 0, 0)),
        scratch_shapes=[pltpu.VMEM((2, past, LANES), F32), pltpu.SemaphoreType.DMA((2,)),
                        pltpu.VMEM((n16 + SUBLANES, LANES), F32)],
    )
    return pl.pallas_call(
        functools.partial(_compress_sample_kernel, past=past, page=page),
        grid_spec=grid_spec,
        out_shape=jax.ShapeDtypeStruct((nb, n16, LANES), F32),
        compiler_params=_cparams(),
        name="nsa_compress_sample",
    )(page_table, lidx, w_lane, cache)


def _split3(x):
    hi = x.astype(BF16)
    r = x - hi.astype(F32)
    mid = r.astype(BF16)
    lo = (r - mid.astype(F32)).astype(BF16)
    return hi, mid, lo


def _select_blocks(imp_cmp, qpos, n_cmp, n_sel, n_sel_pad):
    rows, n16 = imp_cmp.shape
    ci = _iota((n16, n_sel_pad), 0)
    sj = _iota((n16, n_sel_pad), 1)
    span = (SEL_BLOCK + CMP_BLOCK) // CMP_STRIDE - 1
    first = sj * (SEL_BLOCK // CMP_STRIDE) - CMP_BLOCK // CMP_STRIDE + 1
    agg = jnp.where((ci >= first) & (ci < first + span) & (ci < n_cmp) & (sj < n_sel), 1.0, 0.0).astype(BF16)
    imp = sum(jnp.dot(part, agg, preferred_element_type=F32) for part in _split3(imp_cmp))
    blk = _iota((rows, n_sel_pad), 1)
    cur = qpos // SEL_BLOCK
    forced = (blk == 0) | (blk == cur) | (blk == cur - 1)
    imp = jnp.where(forced, FORCE_SCORE, imp)
    imp = jnp.where(blk <= cur, imp, -1.0)
    imp = jnp.where(blk < n_sel, imp, -2.0)
    sel = jnp.zeros((rows, n_sel_pad), F32)
    blk_f = blk.astype(F32)
    for _ in range(min(N_SEL, n_sel)):
        mx = jnp.max(imp, axis=1, keepdims=True)
        first_idx = jnp.min(jnp.where(imp == mx, blk_f, float(n_sel_pad)), axis=1, keepdims=True)
        pick = blk_f == first_idx
        sel = jnp.where(pick, 1.0, sel)
        imp = jnp.where(pick, -3.0, imp)
    return sel


def _cmp_softmax(s, valid):
    s = jnp.where(valid, s, NEG)
    m = jnp.max(s, axis=1, keepdims=True)
    p = jnp.where(valid, jnp.exp(s - m), 0.0)
    return p / jnp.maximum(jnp.sum(p, axis=1, keepdims=True), 1e-30)


def _cmp_prompt_kernel(q_ref, ct_ref, bias_ref, oc_ref, sel_ref, *, tq, n_cmp, n_sel):
    i = pl.program_id(1)
    n16 = ct_ref.shape[0]
    scale = DK_NSA ** -0.5
    kvc = ct_ref[...].astype(BF16)
    qpos = i * tq + _iota((tq, 1), 0)
    n_i = _iota((1, n16), 1)
    valid = (qpos >= n_i * CMP_STRIDE + CMP_BLOCK - 1) & (n_i < n_cmp)
    lane = _iota((tq, LANES), 1)
    imp = jnp.zeros((tq, n16), F32)
    prev = None
    for h in range(H_NSA):
        q = _lane_pick(q_ref[:, LANES * (h // 2):LANES * (h // 2 + 1)], DK_NSA * (h % 2), DK_NSA, 0).astype(BF16)
        s = lax.dot_general(q, kvc, (((1,), (1,)), ((), ())), preferred_element_type=F32) * scale + bias_ref[h]
        p = _cmp_softmax(s, valid)
        imp = imp + p
        o = jnp.dot(p.astype(BF16), kvc, preferred_element_type=F32)
        if h % 2 == 0:
            prev = o
        else:
            oc_ref[:, LANES * (h // 2):LANES * (h // 2 + 1)] = jnp.where(lane < HEAD_V, pltpu.roll(prev, HEAD_V, 1), o)
    sel_ref[...] = _select_blocks(imp, qpos, n_cmp, n_sel, sel_ref.shape[1])


def _cmp_prompt_call(qn, cmp_t, bias_c, batch, seq, tq, n_sel_pad):
    nq = seq // tq
    n16 = seq // CMP_STRIDE
    n_cmp = n16 - CMP_BLOCK // CMP_STRIDE + 1
    n_sel = -(-seq // SEL_BLOCK)
    return pl.pallas_call(
        functools.partial(_cmp_prompt_kernel, tq=tq, n_cmp=n_cmp, n_sel=n_sel),
        grid=(batch, nq),
        in_specs=[pl.BlockSpec((tq, H_NSA * DK_NSA), lambda b, i: (b * nq + i, 0)),
                  pl.BlockSpec((None, n16, LANES), lambda b, i: (b, 0, 0)),
                  pl.BlockSpec((H_NSA, tq, n16), lambda b, i: (0, i, 0))],
        out_specs=[pl.BlockSpec((tq, H_NSA * HEAD_V), lambda b, i: (b * nq + i, 0)),
                   pl.BlockSpec((tq, n_sel_pad), lambda b, i: (b * nq + i, 0))],
        out_shape=[jax.ShapeDtypeStruct((batch * seq, H_NSA * HEAD_V), F32),
                   jax.ShapeDtypeStruct((batch * seq, n_sel_pad), F32)],
        compiler_params=_cparams(),
        name="nsa_cmp_prompt",
    )(qn, cmp_t, bias_c)


def _cmp_sample_kernel(q_ref, ct_ref, bias_ref, oc_ref, sel_ref, *, past, n_cmp, n_sel):
    n16 = ct_ref.shape[0]
    rows = H_NSA * SUBLANES
    scale = DK_NSA ** -0.5
    kvc = ct_ref[...].astype(BF16)
    t_in = _iota((rows, 1), 0) % SUBLANES
    n_i = _iota((1, n16), 1)
    valid = (past + t_in >= n_i * CMP_STRIDE + CMP_BLOCK - 1) & (n_i < n_cmp)
    q = q_ref[...].astype(BF16)
    s = lax.dot_general(q, kvc, (((1,), (1,)), ((), ())), preferred_element_type=F32) * scale + bias_ref[...]
    p = _cmp_softmax(s, valid)
    oc_ref[...] = jnp.dot(p.astype(BF16), kvc, preferred_element_type=F32)
    imp = p[0:SUBLANES]
    for h in range(1, H_NSA):
        imp = imp + p[SUBLANES * h:SUBLANES * (h + 1)]
    qpos = past + _iota((SUBLANES, 1), 0)
    sel_ref[...] = _select_blocks(imp, qpos, n_cmp, n_sel, sel_ref.shape[1])


def _cmp_sample_call(q_st, cmp_t, bias_cs, past, seq_new, n_sel_pad):
    nb = q_st.shape[0]
    rows = H_NSA * SUBLANES
    n16 = (past + seq_new) // CMP_STRIDE
    n_cmp = n16 - CMP_BLOCK // CMP_STRIDE + 1
    n_sel = -(-(past + seq_new) // SEL_BLOCK)
    return pl.pallas_call(
        functools.partial(_cmp_sample_kernel, past=past, n_cmp=n_cmp, n_sel=n_sel),
        grid=(nb,),
        in_specs=[pl.BlockSpec((None, rows, LANES), lambda b: (b, 0, 0)),
                  pl.BlockSpec((None, n16, LANES), lambda b: (b, 0, 0)),
                  pl.BlockSpec((rows, n16), lambda b: (0, 0))],
        out_specs=[pl.BlockSpec((None, rows, LANES), lambda b: (b, 0, 0)),
                   pl.BlockSpec((None, SUBLANES, n_sel_pad), lambda b: (b, 0, 0))],
        out_shape=[jax.ShapeDtypeStruct((nb, rows, LANES), F32),
                   jax.ShapeDtypeStruct((nb, SUBLANES, n_sel_pad), F32)],
        compiler_params=_cparams(),
        name="nsa_cmp_sample",
    )(q_st, cmp_t, bias_cs)


def _sel_add_mask(sel_bf, kpos0, tk, causal_keep=None):
    n_sel_pad = sel_bf.shape[1]
    blk = (kpos0 + _iota((n_sel_pad, tk), 1)) // SEL_BLOCK
    expand = jnp.where(_iota((n_sel_pad, tk), 0) == blk, 1.0, 0.0).astype(BF16)
    hit = jnp.dot(sel_bf, expand, preferred_element_type=F32) > 0.5
    if causal_keep is not None:
        hit = hit & causal_keep
    return jnp.where(hit, 0.0, NEG)


def _nsa_prompt_kernel(q_ref, ks_ref, kw_ref, t_ref, sel_ref, oc_ref, misc_ref, o_ref, ks_sc, kw_sc, mask_sc,
                       *, tq, tk):
    i = pl.program_id(1)

    @pl.when(i == 0)
    def _():
        ks_sc[...] = ks_ref[...].astype(BF16)
        kw_sc[...] = kw_ref[...].astype(BF16)

    scale = DK_NSA ** -0.5
    q0 = i * tq
    cd = q0 // tk
    qpos = q0 + _iota((tq, 1), 0)
    sel_bf = sel_ref[...].astype(BF16)
    n_chunks = mask_sc.shape[0]
    for c in range(n_chunks):
        @pl.when(c <= cd)
        def _(c=c):
            keep = qpos >= c * tk + _iota((1, tk), 1)
            mask_sc[c] = _sel_add_mask(sel_bf, c * tk, tk, keep)

    gates = _sigmoid(misc_ref[...])
    goff = 2 * D_ROPE
    lane = _iota((tq, LANES), 1)
    win_chunks = []
    for cw in (cd - 1, cd):
        kp = cw * tk + _iota((1, tk), 1)
        win_chunks.append((jnp.maximum(cw, 0), cw, (kp <= qpos) & (kp > qpos - WINDOW) & (kp >= 0)))
    for g in range(H_NSA // HEAD_GROUP):
        heads = list(range(HEAD_GROUP * g, HEAD_GROUP * (g + 1)))
        qs = [_lane_pick(q_ref[:, LANES * (h // 2):LANES * (h // 2 + 1)], DK_NSA * (h % 2), DK_NSA, 0).astype(BF16)
              for h in heads]

        def body(c, carry, qs=qs, heads=heads):
            kk = ks_sc[pl.ds(pl.multiple_of(c * tk, tk), tk), :]
            add = mask_sc[c]
            return tuple(_flash_one(cr, q, kk, kk, scale, add=add,
                                    bias=_bias_prompt(t_ref, h, (q0 - c * tk) // LANES, tq, tk))
                         for cr, q, h in zip(carry, qs, heads))

        sel_out = lax.fori_loop(0, cd + 1, body, _flash_init(HEAD_GROUP, tq, LANES))
        win_out = _flash_init(HEAD_GROUP, tq, LANES)
        for cc, cw, keep in win_chunks:
            kk = kw_sc[pl.ds(pl.multiple_of(cc * tk, tk), tk), :]
            win_out = tuple(_flash_one(cr, q, kk, kk, scale, keep=keep,
                                       bias=_bias_prompt(t_ref, h, (q0 - cw * tk) // LANES, tq, tk))
                            for cr, q, h in zip(win_out, qs, heads))
        outs = []
        for h, (_, ls, accs), (_, lw, accw) in zip(heads, sel_out, win_out):
            oc = oc_ref[:, LANES * (h // 2):LANES * (h // 2 + 1)]
            if h % 2 == 0:
                oc = pltpu.roll(oc, HEAD_V, 1)
            g0 = gates[:, goff + N_BRANCH * h:goff + N_BRANCH * h + 1]
            g1 = gates[:, goff + N_BRANCH * h + 1:goff + N_BRANCH * h + 2]
            g2 = gates[:, goff + N_BRANCH * h + 2:goff + N_BRANCH * h + 3]
            outs.append(g0 * oc + g1 * (accs / ls) + g2 * (accw / lw))
        for j in range(0, HEAD_GROUP, 2):
            hp = (heads[j]) // 2
            o_ref[:, LANES * hp:LANES * (hp + 1)] = jnp.where(lane < HEAD_V, pltpu.roll(outs[j], HEAD_V, 1),
                                                              outs[j + 1])


def _nsa_prompt_call(qn, rslc, rwin, t_nsa, sel, oc, misc, batch, seq, tq, tk):
    nq = seq // tq
    n_sel_pad = sel.shape[1]
    row = lambda b, i: (b * nq + i, 0)
    return pl.pallas_call(
        functools.partial(_nsa_prompt_kernel, tq=tq, tk=tk),
        grid=(batch, nq),
        in_specs=[pl.BlockSpec((tq, H_NSA * DK_NSA), row),
                  pl.BlockSpec((seq, LANES), lambda b, i: (b, 0)),
                  pl.BlockSpec((seq, LANES), lambda b, i: (b, 0)),
                  pl.BlockSpec(t_nsa.shape, lambda b, i: (0, 0, 0, 0)),
                  pl.BlockSpec((tq, n_sel_pad), row),
                  pl.BlockSpec((tq, H_NSA * HEAD_V), row),
                  pl.BlockSpec((tq, LANES), row)],
        out_specs=pl.BlockSpec((tq, H_NSA * HEAD_V), row),
        out_shape=jax.ShapeDtypeStruct((batch * seq, H_NSA * HEAD_V), F32),
        scratch_shapes=[pltpu.VMEM((seq, LANES), BF16), pltpu.VMEM((seq, LANES), BF16),
                        pltpu.VMEM((seq // tk, tq, tk), F32)],
        compiler_params=_cparams(),
        name="nsa_attn_prompt",
    )(qn, rslc, rwin, t_nsa, sel, oc, misc)


def _nsa_sample_kernel(pt_ref, l_ref, q_ref, news_ref, neww_ref, win_ref, ts_ref, sel_ref, oc_ref, gt_ref,
                       cache_hbm, o_ref, buf, sem, *, past, page, tk, n_chain):
    slot = _paged_fetch(pt_ref, cache_hbm, l_ref[0], _rows_dst(buf, page), sem, past // page)
    rows = H_NSA * SUBLANES
    scale = DK_NSA ** -0.5
    q = q_ref[...].astype(BF16)
    sel_bf = sel_ref[...].astype(BF16)
    t_in = _iota((rows, 1), 0) % SUBLANES
    qpos = past + t_in
    keep_tail = _iota((1, LANES), 1) <= t_in
    per = past // tk // n_chain

    def body(c, carry):
        out = []
        for j in range(n_chain):
            cc = c + j * per
            kk = buf[slot, pl.ds(pl.multiple_of(cc * tk, tk), tk), :].astype(BF16)
            bias = _bias_stacked(ts_ref, (past - cc * tk) // LANES, tk)
            out.append(_flash_one(carry[j], q, kk, kk, scale, bias=bias, add=_sel_add_mask(sel_bf, cc * tk, tk)))
        return tuple(out)

    carry = lax.fori_loop(0, per, body, _flash_init(n_chain, rows, LANES))
    kt = _tail_rows(news_ref[...], LANES)
    tail = _flash_one(carry[0], q, kt, kt, scale, bias=ts_ref[0], add=_sel_add_mask(sel_bf, past, LANES, keep_tail))
    ls, accs = _flash_merge((tail,) + carry[1:])

    lw = win_ref.shape[0]
    kbase = past - lw
    (carry,) = _flash_init(1, rows, LANES)
    for c in range(lw // tk):
        kk = win_ref[c * tk:(c + 1) * tk, :].astype(BF16)
        kp = kbase + c * tk + _iota((1, tk), 1)
        bias = _bias_stacked(ts_ref, (past - kbase - c * tk) // LANES, tk)
        carry = _flash_one(carry, q, kk, kk, scale, bias=bias, keep=kp > qpos - WINDOW)
    kt = _tail_rows(neww_ref[...], LANES)
    _, lwn, accw = _flash_one(carry, q, kt, kt, scale, bias=ts_ref[0], keep=keep_tail)

    gates = _sigmoid(gt_ref[...])
    o_ref[...] = gates[:, 0:1] * oc_ref[...] + gates[:, 1:2] * (accs / ls) + gates[:, 2:3] * (accw / lwn)


def _nsa_sample_call(q_st, rslc, rwin, state_win, ts_nsa, sel_st, oc_st, gt_st, cache, page_table, lidx, row0,
                     past, tk):
    nb = q_st.shape[0]
    page = cache.shape[2]
    rows = H_NSA * SUBLANES
    lw = state_win.shape[2]
    n_sel_pad = sel_st.shape[2]
    st3 = lambda b, pt, l: (b, 0, 0)
    new = lambda b, pt, l: (row0 // SUBLANES + b, 0)
    grid_spec = pltpu.PrefetchScalarGridSpec(
        num_scalar_prefetch=2,
        grid=(nb,),
        in_specs=[pl.BlockSpec((None, rows, LANES), st3),
                  pl.BlockSpec((SUBLANES, LANES), new),
                  pl.BlockSpec((SUBLANES, LANES), new),
                  pl.BlockSpec((None, None, lw, LANES), lambda b, pt, l: (l[0], b, 0, 0)),
                  pl.BlockSpec(ts_nsa.shape, lambda b, pt, l: (0, 0, 0)),
                  pl.BlockSpec((None, rows, n_sel_pad), st3),
                  pl.BlockSpec((None, rows, LANES), st3),
                  pl.BlockSpec((None, rows, LANES), st3),
                  pl.BlockSpec(memory_space=pl.ANY)],
        out_specs=pl.BlockSpec((None, rows, LANES), st3),
        scratch_shapes=[pltpu.VMEM((2, past, LANES), F32), pltpu.SemaphoreType.DMA((2,))],
    )
    return pl.pallas_call(
        functools.partial(_nsa_sample_kernel, past=past, page=page, tk=tk, n_chain=_n_chain(past, tk)),
        grid_spec=grid_spec,
        out_shape=jax.ShapeDtypeStruct((nb, rows, LANES), F32),
        compiler_params=_cparams(),
        name="nsa_attn_sample",
    )(page_table, lidx, q_st, rslc, rwin, state_win, ts_nsa, sel_st, oc_st, gt_st, cache)


def _post_attn_kernel(l_ref, x_ref, od_ref, ol_ref, on_ref, ga_ref, sh_ref, sc_ref, g_ref, wuv_ref, wo_ref, wr_ref, br_ref,
                      x1_ref, h2_ref, eidx_ref, gate_ref):
    g8, _, d = x_ref.shape
    tm = g8 * SUBLANES
    ol = ol_ref[...]
    parts = [od_ref[...]]
    for hp in range(H_MLA // 2):
        parts.append(jnp.dot(ol[:, 2 * KV_LORA * hp:2 * KV_LORA * (hp + 1)].astype(BF16), wuv_ref[hp],
                             preferred_element_type=F32))
    parts.append(on_ref[...])
    o = jnp.concatenate(parts, axis=1).astype(BF16)
    y = jnp.dot(o, wo_ref[...], preferred_element_type=F32).reshape(g8, SUBLANES, d)
    x1 = x_ref[...] + ga_ref[...] * y
    x1_ref[...] = x1
    h2 = _rms(x1, g_ref[...]) * (1.0 + sc_ref[...]) + sh_ref[...]
    h2 = h2.reshape(tm, d)
    h2_ref[...] = h2
    logits = jnp.dot(h2.astype(BF16), wr_ref[...], preferred_element_type=F32) + br_ref[...]
    lane = _iota((tm, LANES), 1)
    lane_f = lane.astype(F32)
    eidx = jnp.zeros((tm, LANES), F32)
    vals = jnp.full((tm, LANES), NEG, F32)
    work = logits
    for k in range(TOP_K):
        mx = jnp.max(work, axis=1, keepdims=True)
        idx = jnp.min(jnp.where(work == mx, lane_f, float(LANES)), axis=1, keepdims=True)
        eidx = jnp.where(lane == k, idx, eidx)
        vals = jnp.where(lane == k, mx, vals)
        work = jnp.where(lane_f == idx, NEG, work)
    e = jnp.exp(vals - jnp.max(vals, axis=1, keepdims=True))
    gate_ref[...] = e / jnp.sum(e, axis=1, keepdims=True)
    eidx_ref[...] = eidx.astype(jnp.int32)


def _post_attn_call(x3, od, ol, on, modg, g_mlp, wuv, wo, wr, br, lidx, tm):
    n8, _, d = x3.shape
    n = n8 * SUBLANES
    g8 = tm // SUBLANES
    row = lambda i, l: (i, 0)
    lay3 = lambda i, l: (l[0], 0, 0)
    mod = lambda k: pl.BlockSpec((None, None, g8, 1, d), lambda i, l: (l[0], k, i, 0, 0))
    grid_spec = pltpu.PrefetchScalarGridSpec(
        num_scalar_prefetch=1,
        grid=(n // tm,),
        in_specs=[pl.BlockSpec((g8, SUBLANES, d), lambda i, l: (i, 0, 0)),
                  pl.BlockSpec((tm, od.shape[1]), row), pl.BlockSpec((tm, ol.shape[1]), row),
                  pl.BlockSpec((tm, on.shape[1]), row),
                  mod(2), mod(3), mod(4),
                  pl.BlockSpec((None, 1, d), lay3),
                  pl.BlockSpec((None, H_MLA // 2, 2 * KV_LORA, 2 * HEAD_V), lambda i, l: (l[0], 0, 0, 0)),
                  pl.BlockSpec((None,) + wo.shape[1:], lay3),
                  pl.BlockSpec((None, d, LANES), lay3),
                  pl.BlockSpec((None, 1, LANES), lay3)],
        out_specs=[pl.BlockSpec((g8, SUBLANES, d), lambda i, l: (i, 0, 0)),
                   pl.BlockSpec((tm, d), row), pl.BlockSpec((tm, LANES), row), pl.BlockSpec((tm, LANES), row)],
    )
    return pl.pallas_call(
        _post_attn_kernel,
        grid_spec=grid_spec,
        out_shape=[jax.ShapeDtypeStruct((n8, SUBLANES, d), F32), jax.ShapeDtypeStruct((n, d), F32),
                   jax.ShapeDtypeStruct((n, LANES), jnp.int32), jax.ShapeDtypeStruct((n, LANES), F32)],
        compiler_params=_cparams(),
        name="out_proj_router",
    )(lidx, x3, od, ol, on, modg, modg, modg, g_mlp, wuv, wo, wr, br)


def _moe_row_copies(h_hbm, src_ref, blk, xbuf, sem, slot, start):
    def body(r, _):
        tok = src_ref[blk * MOE_ROWS + r]
        cp = pltpu.make_async_copy(h_hbm.at[pl.ds(tok, 1)], xbuf.at[slot, pl.ds(r, 1)], sem.at[slot])
        if start:
            cp.start()
        else:
            cp.wait()
        return 0
    lax.fori_loop(0, MOE_ROWS, body, 0, unroll=8)


def _moe_kernel(l_ref, be_ref, src_ref, h_hbm, wgu_ref, bgu_ref, wd_ref, bd_ref, o_ref, xbuf, sem, wgu_sc, wd_sc,
                *, d_ff):
    i = pl.program_id(0)
    nb = pl.num_programs(0)
    slot = i % 2

    @pl.when(i == 0)
    def _():
        _moe_row_copies(h_hbm, src_ref, 0, xbuf, sem, 0, True)

    @pl.when(i + 1 < nb)
    def _():
        _moe_row_copies(h_hbm, src_ref, i + 1, xbuf, sem, 1 - slot, True)

    @pl.when((i == 0) | (be_ref[i] != be_ref[jnp.maximum(i - 1, 0)]))
    def _():
        wgu_sc[...] = wgu_ref[...].astype(BF16)
        wd_sc[...] = wd_ref[...].astype(BF16)

    _moe_row_copies(h_hbm, src_ref, i, xbuf, sem, slot, False)
    x = xbuf[slot].astype(BF16)
    gu = jnp.dot(x, wgu_sc[...], preferred_element_type=F32) + bgu_ref[...]
    g = jnp.minimum(gu[:, :d_ff], SWIGLU_LIMIT)
    u = jnp.clip(gu[:, d_ff:], -SWIGLU_LIMIT, SWIGLU_LIMIT)
    a = (u + 1.0) * g * _sigmoid(SWIGLU_ALPHA * g)
    o_ref[...] = jnp.dot(a.astype(BF16), wd_sc[...], preferred_element_type=F32) + bd_ref[...]


def _moe_call(h2, w_gate_up, b_gate_up, w_down, b_down, blk_e, src_tok, lidx):
    n, d = h2.shape
    depth, n_exp, _, ff2 = w_gate_up.shape
    d_ff = ff2 // 2
    n_blocks = blk_e.shape[0]
    grid_spec = pltpu.PrefetchScalarGridSpec(
        num_scalar_prefetch=3,
        grid=(n_blocks,),
        in_specs=[pl.BlockSpec(memory_space=pl.ANY),
                  pl.BlockSpec((None, None, d, ff2), lambda i, l, be, src: (l[0], be[i], 0, 0)),
                  pl.BlockSpec((None, None, 1, ff2), lambda i, l, be, src: (l[0], be[i], 0, 0)),
                  pl.BlockSpec((None, None, d_ff, d), lambda i, l, be, src: (l[0], be[i], 0, 0)),
                  pl.BlockSpec((None, None, 1, d), lambda i, l, be, src: (l[0], be[i], 0, 0))],
        out_specs=pl.BlockSpec((MOE_ROWS, d), lambda i, l, be, src: (i, 0)),
        scratch_shapes=[pltpu.VMEM((2, MOE_ROWS, d), F32), pltpu.SemaphoreType.DMA((2,)),
                        pltpu.VMEM((d, ff2), BF16), pltpu.VMEM((d_ff, d), BF16)],
    )
    return pl.pallas_call(
        functools.partial(_moe_kernel, d_ff=d_ff),
        grid_spec=grid_spec,
        out_shape=jax.ShapeDtypeStruct((n_blocks * MOE_ROWS, d), F32),
        compiler_params=_cparams(),
        name="moe_experts",
    )(lidx, blk_e, src_tok, h2, w_gate_up, b_gate_up.reshape(depth, n_exp, 1, ff2), w_down,
      b_down.reshape(depth, n_exp, 1, d))


def _combine_row_copies(yd_hbm, dest_ref, blk, ybuf, sem, slot, tc, start):
    def body(r, _):
        for k in range(TOP_K):
            row = dest_ref[(blk * tc + r) * TOP_K + k]
            cp = pltpu.make_async_copy(yd_hbm.at[pl.ds(row, 1)], ybuf.at[slot, k, pl.ds(r, 1)], sem.at[slot])
            if start:
                cp.start()
            else:
                cp.wait()
        return 0
    lax.fori_loop(0, tc, body, 0, unroll=4)


def _combine_kernel(l_ref, dest_ref, x_ref, ga_ref, gate_ref, yd_hbm, o_ref, ybuf, sem, *, tc):
    i = pl.program_id(0)
    nb = pl.num_programs(0)
    slot = i % 2

    @pl.when(i == 0)
    def _():
        _combine_row_copies(yd_hbm, dest_ref, 0, ybuf, sem, 0, tc, True)

    @pl.when(i + 1 < nb)
    def _():
        _combine_row_copies(yd_hbm, dest_ref, i + 1, ybuf, sem, 1 - slot, tc, True)

    _combine_row_copies(yd_hbm, dest_ref, i, ybuf, sem, slot, tc, False)
    g8, _, d = x_ref.shape
    gate = gate_ref[...]
    acc = ybuf[slot, 0] * gate[:, 0:1]
    for k in range(1, TOP_K):
        acc = acc + ybuf[slot, k] * gate[:, k:k + 1]
    o_ref[...] = x_ref[...] + ga_ref[...] * acc.reshape(g8, SUBLANES, d)


def _combine_call(x3, modg, gate, yd, dest, lidx, tc):
    n8, _, d = x3.shape
    n = n8 * SUBLANES
    g8 = tc // SUBLANES
    grid_spec = pltpu.PrefetchScalarGridSpec(
        num_scalar_prefetch=2,
        grid=(n // tc,),
        in_specs=[pl.BlockSpec((g8, SUBLANES, d), lambda i, l, dst: (i, 0, 0)),
                  pl.BlockSpec((None, None, g8, 1, d), lambda i, l, dst: (l[0], 5, i, 0, 0)),
                  pl.BlockSpec((tc, LANES), lambda i, l, dst: (i, 0)),
                  pl.BlockSpec(memory_space=pl.ANY)],
        out_specs=pl.BlockSpec((g8, SUBLANES, d), lambda i, l, dst: (i, 0, 0)),
        scratch_shapes=[pltpu.VMEM((2, TOP_K, tc, d), F32), pltpu.SemaphoreType.DMA((2,))],
    )
    return pl.pallas_call(
        functools.partial(_combine_kernel, tc=tc),
        grid_spec=grid_spec,
        out_shape=jax.ShapeDtypeStruct((n8, SUBLANES, d), F32),
        compiler_params=_cparams(),
        name="moe_combine",
    )(lidx, dest, x3, modg, gate, yd)


def _final_norm_kernel(x_ref, g_ref, o_ref):
    o_ref[...] = _rms(x_ref[...], g_ref[...])


def _final_norm_call(x, g, tm):
    n, d = x.shape
    return pl.pallas_call(
        _final_norm_kernel,
        grid=(n // tm,),
        in_specs=[pl.BlockSpec((tm, d), lambda i: (i, 0)), pl.BlockSpec((1, d), lambda i: (0, 0))],
        out_specs=pl.BlockSpec((tm, d), lambda i: (i, 0)),
        out_shape=jax.ShapeDtypeStruct((n, d), F32),
        compiler_params=_cparams(),
        name="final_norm",
    )(x, g.reshape(1, d))


def _rel_bucket(dist):
    dist = jnp.maximum(dist, 0)
    exact = N_BUCKETS // 2
    large = exact + (jnp.log(jnp.maximum(dist, 1).astype(F32) / exact)
                     / math.log(MAX_DISTANCE / exact) * (N_BUCKETS - exact)).astype(jnp.int32)
    return jnp.where(dist < exact, dist, jnp.minimum(large, N_BUCKETS - 1))


def _bias_tables(rel_bias, seq, past, seq_new):
    def lookup(tab, dist):
        bucket = _rel_bucket(dist)[None]
        cols = tab.T.reshape((tab.shape[1],) + (1,) * dist.ndim + (N_BUCKETS,))
        out = jnp.zeros((tab.shape[1],) + dist.shape, F32)
        for b in range(N_BUCKETS):
            out = jnp.where(bucket == b, cols[..., b], out)
        return out

    t = jnp.arange(LANES, dtype=jnp.int32)
    toep = lookup(rel_bias, t[None, :, None] - t[None, None, :]
                  + LANES * jnp.arange(3, dtype=jnp.int32)[:, None, None])
    t_diff, t_nsa = toep[:H_DIFF], toep[H_DIFF:]
    stack = lambda tt: tt[:, :, :SUBLANES, :].transpose(1, 0, 2, 3).reshape(3, -1, LANES)
    nsa_tab = rel_bias[:, H_DIFF:]
    n16p = seq // CMP_STRIDE
    cmp_end = jnp.arange(n16p, dtype=jnp.int32) * CMP_STRIDE + CMP_BLOCK - 1
    bias_cp = lookup(nsa_tab, jnp.arange(seq, dtype=jnp.int32)[:, None] - cmp_end[None, :])
    n16s = (past + seq_new) // CMP_STRIDE
    cmp_end = jnp.arange(n16s, dtype=jnp.int32) * CMP_STRIDE + CMP_BLOCK - 1
    bias_cs = lookup(nsa_tab, past + jnp.arange(seq_new, dtype=jnp.int32)[:, None] - cmp_end[None, :])
    return t_diff, t_nsa, stack(t_diff), stack(t_nsa), bias_cp, bias_cs.reshape(H_NSA * seq_new, n16s)


def _rope_tables(pos):
    half = D_ROPE // 2
    inv = ROPE_BASE ** (-jnp.arange(half, dtype=F32) / half)
    ang = pos.astype(F32)[:, None] * inv
    cos, sin = jnp.cos(ang), jnp.sin(ang)
    rep = LANES // D_ROPE
    return jnp.tile(jnp.concatenate([cos, cos], axis=1), (1, rep)), jnp.tile(jnp.concatenate([-sin, sin], axis=1), (1, rep))


def _routing(eidx, n_exp):
    n = eidx.shape[0]
    n_assign = n * TOP_K
    flat_e = eidx[:, :TOP_K].reshape(-1)
    flat_t = jnp.arange(n_assign, dtype=jnp.int32) // TOP_K
    order = jnp.argsort(flat_e)
    st = flat_t[order]
    counts = jnp.sum(flat_e[:, None] == jnp.arange(n_exp, dtype=jnp.int32)[None, :], axis=0, dtype=jnp.int32)
    padded = (counts + MOE_ROWS - 1) // MOE_ROWS * MOE_ROWS
    pend = jnp.cumsum(padded)
    pstart = pend - padded
    cstart = jnp.cumsum(counts) - counts
    rank = jnp.argsort(order).astype(jnp.int32)
    dest = pstart[flat_e] + rank - cstart[flat_e]
    n_blocks = -(-(n_assign + n_exp * (MOE_ROWS - 1)) // MOE_ROWS)
    blk_start = jnp.arange(n_blocks, dtype=jnp.int32) * MOE_ROWS
    blk_e = jnp.minimum(jnp.sum(pend[None, :] <= blk_start[:, None], axis=1, dtype=jnp.int32), n_exp - 1)
    slot = jnp.arange(n_blocks * MOE_ROWS, dtype=jnp.int32)
    e_of = jnp.repeat(blk_e, MOE_ROWS)
    within = slot - pstart[e_of]
    ok = (within >= 0) & (within < counts[e_of])
    src_tok = jnp.where(ok, st[jnp.clip(cstart[e_of] + within, 0, n_assign - 1)], 0).astype(jnp.int32)
    return blk_e, src_tok, dest


def _stack_heads(a, nb, heads, width):
    return a.reshape(nb, SUBLANES, heads, width).transpose(0, 2, 1, 3).reshape(nb, heads * SUBLANES, width)


def _unstack_heads(a, nb, heads):
    width = a.shape[-1]
    return a.reshape(nb, heads, SUBLANES, width).transpose(0, 2, 1, 3).reshape(nb * SUBLANES, heads * width)


def _pad_lanes(a, before, total):
    pad = [(0, 0)] * (a.ndim - 1) + [(before, total - before - a.shape[-1])]
    return jnp.pad(a, pad)


def kernel(x_prompt, x_sample, cache_diff, cache_mla, cache_cmp, cache_slc, state_win, page_table, c_prompt, c_sample,
           w_mod, b_mod, g_attn, g_mlp, w_in, g_q_lat, g_kv_lat, w_uq, w_uk, w_uv, diff_lambda, g_diff_sub, w_cmp,
           w_out, w_router, b_router, w_gate_up, b_gate_up, w_down, b_down, rel_bias, g_final):
    batch, seq, d = x_prompt.shape
    nb, seq_new, _ = x_sample.shape
    assert seq_new == SUBLANES
    depth = w_mod.shape[0]
    n_exp = w_router.shape[2]
    page = cache_diff.shape[2]
    past = page_table.shape[1] * page
    n_p, n_s = batch * seq, nb * seq_new
    n = n_p + n_s
    tm = _pick_tile(n, 256)
    tq, tk = 256, 512
    assert seq % tk == 0 and past % tk == 0 and WINDOW <= tk and past >= WINDOW and seq >= WINDOW
    assert n_p % tm == 0
    page_table = page_table.astype(jnp.int32)
    cache_mla_t = jnp.swapaxes(cache_mla, 2, 3)

    o = [0]
    for w in (256, 256, 32, 32, 64, 384, 128, 32, 768, 64, 64, 64, 64, 64, 64, 36):
        o.append(o[-1] + w)
    kr0, qn0, gt0 = o[7], o[8], o[15]
    w_in_p = jnp.concatenate(
        [w_in[:, :, :kr0], w_in[:, :, qn0:gt0], w_in[:, :, kr0:qn0],
         w_in[:, :, kr0 + D_ROPE // 2:qn0], w_in[:, :, kr0:kr0 + D_ROPE // 2], w_in[:, :, gt0:],
         jnp.zeros((depth, d, LANES - 2 * D_ROPE - H_NSA * N_BRANCH), F32)], axis=2).astype(BF16)
    dq = D_NOPE + D_ROPE
    wq_nope = w_uq[..., :D_NOPE].reshape(depth, Q_LORA, H_MLA * D_NOPE).astype(BF16)
    wq_rope = w_uq[..., D_NOPE:].reshape(depth, Q_LORA, H_MLA * D_ROPE).astype(BF16)
    wq_rope_sw = jnp.concatenate([w_uq[..., D_NOPE + D_ROPE // 2:], w_uq[..., D_NOPE:D_NOPE + D_ROPE // 2]],
                                 axis=-1).reshape(depth, Q_LORA, H_MLA * D_ROPE).astype(BF16)
    del dq
    w_ukt = w_uk.transpose(0, 2, 3, 1).astype(BF16)
    w_uvh = w_uv.transpose(0, 2, 1, 3).reshape(depth, H_MLA // 2, 2, KV_LORA, HEAD_V)
    zero = jnp.zeros_like(w_uvh[:, :, 0])
    w_uvh = jnp.concatenate([jnp.concatenate([w_uvh[:, :, 0], zero], axis=3),
                             jnp.concatenate([zero, w_uvh[:, :, 1]], axis=3)], axis=2).astype(BF16)
    w_out_b = w_out.astype(BF16)
    w_r = _pad_lanes(w_router, 0, LANES).astype(BF16)
    b_r = jnp.concatenate([b_router, jnp.full((depth, LANES - n_exp), NEG, F32)], axis=1).reshape(depth, 1, LANES)
    g_diff128 = jnp.tile(g_diff_sub, (1, LANES // HEAD_V)).reshape(depth, 1, LANES)
    lam_init = jnp.asarray([0.8 - 0.6 * math.exp(-0.3 * l) for l in range(depth)], F32).reshape(depth, 1, 1)
    w_lane = jnp.concatenate([jnp.broadcast_to(w_cmp[:, 0, :, None], (depth, CMP_BLOCK, DK_NSA)),
                              jnp.broadcast_to(w_cmp[:, 1, :, None], (depth, CMP_BLOCK, HEAD_V))], axis=2)
    t_diff, t_nsa, ts_diff, ts_nsa, bias_cp, bias_cs = _bias_tables(rel_bias, seq, past, seq_new)
    pos = jnp.concatenate([jnp.tile(jnp.arange(seq, dtype=jnp.int32), batch),
                           jnp.tile(past + jnp.arange(seq_new, dtype=jnp.int32), nb)])
    rope_c, rope_s = _rope_tables(pos)
    n_sel_p = -(-(-(-seq // SEL_BLOCK)) // LANES) * LANES
    n_sel_s = -(-(-(-(past + seq_new) // SEL_BLOCK)) // LANES) * LANES

    m_rows = -(-(batch + nb) // SUBLANES) * SUBLANES
    c_all = jnp.pad(jnp.concatenate([c_prompt, c_sample], axis=0), ((0, m_rows - batch - nb), (0, 0)))
    mod = _mod_call(c_all, w_mod, b_mod).reshape(depth, m_rows, 6, d)
    modg = jnp.concatenate([jnp.repeat(mod[:, :batch], seq // SUBLANES, axis=1), mod[:, batch:batch + nb]], axis=1)
    modg = modg.transpose(0, 2, 1, 3).reshape(depth, 6, n // SUBLANES, 1, d)

    x3 = jnp.concatenate([x_prompt.reshape(n_p, d), x_sample.reshape(n_s, d)], axis=0).reshape(n // SUBLANES, SUBLANES, d)
    g_attn3, g_mlp3 = g_attn.reshape(depth, 1, d), g_mlp.reshape(depth, 1, d)
    g_q3, g_kv3 = g_q_lat.reshape(depth, 1, Q_LORA), g_kv_lat.reshape(depth, 1, KV_LORA)

    rows_out = {k: [] for k in ("diff", "mla", "cmp", "slc", "win")}
    for l in range(depth):
        lidx = jnp.full((1,), l, jnp.int32)
        qd, rdiff, cq, ckv, qn, rcmp, rslc, rwin, misc = _proj_call(x3, modg, g_attn3, w_in_p, lidx, tm)
        qabs, rmla = _mla_prep_call(cq, ckv, misc, rope_c, rope_s, g_q3, g_kv3, wq_nope, wq_rope, wq_rope_sw, w_ukt,
                                    lidx, tm)
        for k, r in (("diff", rdiff), ("mla", rmla), ("cmp", rcmp), ("slc", rslc), ("win", rwin)):
            rows_out[k].append(r)

        od_p = _diff_prompt_call(qd, rdiff, t_diff, diff_lambda, lam_init, g_diff128, lidx, batch, seq, tq, tk)
        ol_p = _mla_prompt_call(qabs, rmla, batch, seq, tq, tk)
        cmp_p = _compress_prompt_call(rcmp, w_lane, lidx, batch, seq)
        oc_p, sel_p = _cmp_prompt_call(qn, cmp_p, bias_cp, batch, seq, tq, n_sel_p)
        on_p = _nsa_prompt_call(qn, rslc, rwin, t_nsa, sel_p, oc_p, misc, batch, seq, tq, tk)

        hq = H_DIFF * DK_DIFF
        q1s = _pad_lanes(_stack_heads(qd[n_p:, :hq], nb, H_DIFF, DK_DIFF), 0, LANES)
        q2s = _pad_lanes(_stack_heads(qd[n_p:, hq:], nb, H_DIFF, DK_DIFF), DK_DIFF, LANES)
        od_s = _diff_sample_call(jnp.concatenate([q1s, q2s], axis=1), rdiff, ts_diff, diff_lambda, lam_init, g_diff128,
                                 cache_diff, page_table, lidx, n_p, past, tk)
        od_s = _unstack_heads(od_s[..., HEAD_V:], nb, H_DIFF)
        ol_s = _mla_sample_call(_stack_heads(qabs[n_p:], nb, H_MLA, MLA_W), rmla, cache_mla_t, page_table, lidx, n_p,
                                past, tk)
        ol_s = _unstack_heads(ol_s, nb, H_MLA)
        qn_s = _pad_lanes(_stack_heads(qn[n_p:], nb, H_NSA, DK_NSA), 0, LANES)
        cmp_s = _compress_sample_call(w_lane, cache_cmp, page_table, lidx, nb, past)
        oc_s, sel_s = _cmp_sample_call(qn_s, cmp_s, bias_cs, past, seq_new, n_sel_s)
        gt_s = _pad_lanes(_stack_heads(misc[n_p:, 2 * D_ROPE:2 * D_ROPE + H_NSA * N_BRANCH], nb, H_NSA, N_BRANCH),
                          0, LANES)
        on_s = _nsa_sample_call(qn_s, rslc, rwin, state_win, ts_nsa, jnp.tile(sel_s, (1, H_NSA, 1)), oc_s, gt_s,
                                cache_slc, page_table, lidx, n_p, past, tk)
        on_s = _unstack_heads(on_s[..., HEAD_V:], nb, H_NSA)

        od = jnp.concatenate([od_p, od_s], axis=0)
        ol = jnp.concatenate([ol_p, ol_s], axis=0)
        on = jnp.concatenate([on_p, on_s], axis=0)
        x3, h2, eidx, gate = _post_attn_call(x3, od, ol, on, modg, g_mlp3, w_uvh, w_out_b, w_r, b_r, lidx, tm)
        blk_e, src_tok, dest = _routing(eidx, n_exp)
        yd = _moe_call(h2, w_gate_up, b_gate_up, w_down, b_down, blk_e, src_tok, lidx)
        x3 = _combine_call(x3, modg, gate, yd, dest, lidx, _pick_tile(n, 64))

    y = _final_norm_call(x3.reshape(n, d), g_final, tm)
    y_prompt = y[:n_p].reshape(batch, seq, d)
    y_sample = y[n_p:].reshape(nb, seq_new, d)
    stk = lambda k, sl, shape: jnp.stack([r[sl] for r in rows_out[k]]).reshape((depth,) + shape)
    outs_p = [stk(k, slice(0, n_p), (batch, seq, -1)) for k in ("diff", "mla", "cmp", "slc", "win")]
    outs_s = [stk(k, slice(n_p, n), (nb, seq_new, -1)) for k in ("diff", "mla", "cmp", "slc", "win")]
    keep_p = min(WINDOW, seq)
    win_p = outs_p[4][:, :, seq - keep_p:]
    keep_s = state_win.shape[2]
    win_s = jnp.concatenate([state_win, outs_s[4]], axis=2)[:, :, seq_new:seq_new + keep_s]
    return (y_prompt, y_sample, outs_p[0], outs_p[1], outs_p[2], outs_p[3], win_p,
            outs_s[0], outs_s[1], outs_s[2], outs_s[3], win_s)
```
